```python
import math
import jax, jax.numpy as jnp
from jax import lax
import numpy as np

D_MODEL = 1024
BATCH = 4
SEQ = 8192
DEPTH = 4

D_MIX = D_MODEL
N_MIXERS = 4
D_GROUP = D_MIX // N_MIXERS
HG_HEADS = 4
HG_DK = D_GROUP // HG_HEADS
HG_CHUNK = 64
LB_FLOOR = 1e-30
CONV_WIDTH = 31
S5_CH = 16
S5_GROUPS = D_GROUP // S5_CH
S5_STATE = 64
S5_DT_MIN = 1e-3
S5_DT_MAX = 1e-1
ATT_HEADS = 4
ATT_DH = D_GROUP // ATT_HEADS
ATT_PATTERNS = ((128, 1), (512, 4), (2048, 16))
ATT_BLOCK = 128
REL_BUCKETS = 32
REL_MAX_DIST = 2048
NEG_BIG = -1e30
D_FF = 4 * D_MODEL
N_IN_SLICES = 10
D_IN = N_IN_SLICES * D_GROUP
EPS = 1e-6

kernel_name = "hybrid_hgrn2_conv_s5_dilated_attn"


def rms_norm(x, g):
    xf = x.astype(jnp.float32)
    y = xf * lax.rsqrt(jnp.mean(xf * xf, axis=-1, keepdims=True) + EPS)
    return (y * g.astype(jnp.float32)).astype(x.dtype)


def hgrn2_mixer(q, fz, iv, g, lb, norm_g):
    bsz, s, _ = q.shape
    f32 = jnp.float32
    lb = jnp.clip(lb.astype(f32), 0.0, 1.0 - 1e-6)
    fz = fz.astype(f32)
    log_f = jnp.logaddexp(jnp.log(jnp.maximum(lb, LB_FLOOR)), jnp.log1p(-lb) + jax.nn.log_sigmoid(fz))
    k = (1.0 - lb) * jax.nn.sigmoid(-fz)
    nc = s // HG_CHUNK

    def to_chunks(t):
        t = t.astype(f32).reshape(bsz, nc, HG_CHUNK, HG_HEADS, HG_DK)
        return t.transpose(1, 0, 3, 2, 4)

    causal = jnp.tril(jnp.ones((HG_CHUNK, HG_CHUNK), dtype=bool))

    def step(state, inp):
        qc, lfc, kc, vc = inp
        gcum = jnp.cumsum(lfc, axis=2)
        o_inter = jnp.einsum('bhtk,bhkv->bhtv', qc * jnp.exp(gcum), state)
        diff = gcum[:, :, :, None, :] - gcum[:, :, None, :, :]
        decay = jnp.exp(jnp.where(causal[:, :, None], diff, NEG_BIG))
        scores = jnp.einsum('bhtk,bhsk,bhtsk->bhts', qc, kc, decay)
        o_intra = jnp.einsum('bhts,bhsv->bhtv', scores, vc)
        g_last = gcum[:, :, -1:, :]
        state = (jnp.exp(g_last[:, :, 0, :])[..., None] * state
                 + jnp.einsum('bhsk,bhsv->bhkv', kc * jnp.exp(g_last - gcum), vc))
        return state, o_inter + o_intra

    s0 = jnp.zeros((bsz, HG_HEADS, HG_DK, HG_DK), f32)
    _, o = lax.scan(step, s0, (to_chunks(q), to_chunks(log_f), to_chunks(k), to_chunks(iv)))
    o = o.transpose(1, 0, 3, 2, 4).reshape(bsz, s, HG_HEADS, HG_DK)
    o = rms_norm(o, norm_g.reshape(HG_HEADS, HG_DK)).reshape(bsz, s, D_GROUP)
    return (o * jax.nn.silu(g.astype(f32))).astype(q.dtype)


def conformer_conv_mixer(val, gate, conv_w, conv_b, ln_g, ln_b):
    h = val * jax.nn.sigmoid(gate)
    h = lax.conv_general_dilated(h, conv_w[:, None, :], window_strides=(1,),
                                 padding=((CONV_WIDTH - 1, 0),),
                                 dimension_numbers=('NWC', 'WIO', 'NWC'),
                                 feature_group_count=D_GROUP) + conv_b
    hf = h.astype(jnp.float32)
    mu = jnp.mean(hf, axis=-1, keepdims=True)
    var = jnp.mean(jnp.square(hf - mu), axis=-1, keepdims=True)
    hf = (hf - mu) * lax.rsqrt(var + EPS) * ln_g.astype(jnp.float32) + ln_b.astype(jnp.float32)
    return jax.nn.silu(hf).astype(val.dtype)


def _complex_affine_combine(e1, e2):
    a1r, a1i, b1r, b1i = e1
    a2r, a2i, b2r, b2i = e2
    ar = a2r * a1r - a2i * a1i
    ai = a2r * a1i + a2i * a1r
    br = a2r * b1r - a2i * b1i + b2r
    bi = a2r * b1i + a2i * b1r + b2i
    return (ar, ai, br, bi)


def s5_mixer(u, lam_re, lam_im, b_re, b_im, c_re, c_im, d_skip, log_dt, w_glu):
    bsz, s, _ = u.shape
    f32 = jnp.float32
    uf = u.astype(f32)
    ug = uf.reshape(bsz, s, S5_GROUPS, S5_CH)
    dt = jnp.exp(log_dt.astype(f32))[:, None]
    lr = lam_re.astype(f32)
    li = lam_im.astype(f32)
    mag = jnp.exp(lr * dt)
    a_re = mag * jnp.cos(li * dt)
    a_im = mag * jnp.sin(li * dt)
    den = lr * lr + li * li
    coef_re = ((a_re - 1.0) * lr + a_im * li) / den
    coef_im = (a_im * lr - (a_re - 1.0) * li) / den
    br = b_re.astype(f32)
    bi = b_im.astype(f32)
    bbar_re = coef_re[..., None] * br - coef_im[..., None] * bi
    bbar_im = coef_re[..., None] * bi + coef_im[..., None] * br
    bu_re = jnp.einsum('bsgc,gpc->bsgp', ug, bbar_re)
    bu_im = jnp.einsum('bsgc,gpc->bsgp', ug, bbar_im)
    shape = bu_re.shape
    elems = (jnp.broadcast_to(a_re, shape), jnp.broadcast_to(a_im, shape), bu_re, bu_im)
    _, _, h_re, h_im = lax.associative_scan(_complex_affine_combine, elems, axis=1)
    y = (jnp.einsum('bsgp,gcp->bsgc', h_re, c_re.astype(f32))
         - jnp.einsum('bsgp,gcp->bsgc', h_im, c_im.astype(f32)))
    y = y.reshape(bsz, s, D_GROUP) + d_skip.astype(f32) * uf
    y = jax.nn.gelu(y)
    y = y * jax.nn.sigmoid(y @ w_glu.astype(f32))
    return y.astype(u.dtype)


def t5_bucket(dist):
    max_exact = REL_BUCKETS // 2
    is_small = dist < max_exact
    distf = jnp.maximum(dist, max_exact).astype(jnp.float32)
    large = max_exact + (jnp.log(distf / max_exact) / math.log(REL_MAX_DIST / max_exact)
                         * (REL_BUCKETS - max_exact)).astype(jnp.int32)
    large = jnp.minimum(large, REL_BUCKETS - 1)
    return jnp.where(is_small, dist, large)


def dilated_band_attention(q, k, v, rel_bias, window, dilation):
    bsz, nh, s, dh = q.shape
    f32 = jnp.float32
    span = window // dilation
    n_sub = s // dilation
    nb = -(-n_sub // ATT_BLOCK)
    n_pad = nb * ATT_BLOCK

    def strided_blocks(t):
        t = t.reshape(bsz, nh, n_sub, dilation, dh).transpose(0, 1, 3, 2, 4)
        t = jnp.pad(t, ((0, 0), (0, 0), (0, 0), (0, n_pad - n_sub), (0, 0)))
        return t.reshape(bsz, nh, dilation, nb, ATT_BLOCK, dh)

    def band(t):
        prev = jnp.pad(t, ((0, 0), (0, 0), (0, 0), (1, 0), (0, 0), (0, 0)))[:, :, :, :-1]
        return jnp.concatenate([prev, t], axis=4)

    qb = strided_blocks(q)
    kb = band(strided_blocks(k))
    vb = band(strided_blocks(v))
    i = jnp.arange(ATT_BLOCK)[:, None]
    j = jnp.arange(2 * ATT_BLOCK)[None, :]
    delta = i + ATT_BLOCK - j
    blk = jnp.arange(nb)[:, None, None]
    valid = (delta >= 0) & (delta <= span) & (blk * ATT_BLOCK + j - ATT_BLOCK >= 0)
    bucket = t5_bucket(jnp.maximum(delta, 0) * dilation)
    bias = jnp.transpose(rel_bias[bucket], (2, 0, 1)).astype(f32)
    logits = jnp.einsum('bhrnqd,bhrnkd->bhrnqk', qb, kb).astype(f32) + bias[:, None, None]
    logits = jnp.where(valid, logits, NEG_BIG)
    m = jnp.max(logits, axis=-1, keepdims=True)
    p = jnp.exp(logits - m)
    den = jnp.sum(p, axis=-1)
    o = jnp.einsum('bhrnqk,bhrnkd->bhrnqd', p, vb.astype(f32)) / den[..., None]
    lse = m[..., 0] + jnp.log(den)

    def unstride(t):
        t = t.reshape(bsz, nh, dilation, n_pad, *t.shape[5:])[:, :, :, :n_sub]
        t = jnp.moveaxis(t, 2, 3)
        return t.reshape(bsz, nh, s, *t.shape[4:])

    return unstride(o), unstride(lse)


def dilated_attention_mixer(q, k, v, q_g, k_g, rel_bias):
    bsz, s, _ = q.shape

    def heads(t):
        return t.reshape(bsz, s, ATT_HEADS, ATT_DH).transpose(0, 2, 1, 3)

    qh = rms_norm(heads(q), q_g) * (ATT_DH ** -0.5)
    kh = rms_norm(heads(k), k_g)
    vh = heads(v)
    outs = []
    lses = []
    for window, dilation in ATT_PATTERNS:
        o_p, lse_p = dilated_band_attention(qh, kh, vh, rel_bias, window, dilation)
        outs.append(o_p)
        lses.append(lse_p)
    w = jax.nn.softmax(jnp.stack(lses, axis=0), axis=0)
    o = jnp.einsum('pbhs,pbhsd->bhsd', w, jnp.stack(outs, axis=0))
    return o.transpose(0, 2, 1, 3).reshape(bsz, s, D_GROUP).astype(q.dtype)


def setup_inputs(seed: int = 0) -> dict:
    key = jax.random.key(seed)
    ks = jax.random.split(key, 26)
    f32 = jnp.float32

    def nrm(k, shape, scale):
        return jax.random.normal(k, shape, f32) * scale

    out_scale = (2 * DEPTH) ** -0.5
    lam_im_base = math.pi * jnp.arange(S5_STATE, dtype=f32)
    return {
        "x": nrm(ks[0], (BATCH, SEQ, D_MODEL), 1.0),
        "norm_mix_g": 1.0 + nrm(ks[1], (DEPTH, D_MODEL), 0.02),
        "w_in": nrm(ks[2], (DEPTH, D_MODEL, D_IN), D_MODEL ** -0.5),
        "hgrn_lb_logits": nrm(ks[3], (DEPTH, D_GROUP), 0.1),
        "hgrn_norm_g": 1.0 + nrm(ks[4], (DEPTH, D_GROUP), 0.02),
        "conv_w": nrm(ks[5], (DEPTH, CONV_WIDTH, D_GROUP), CONV_WIDTH ** -0.5),
        "conv_b": nrm(ks[6], (DEPTH, D_GROUP), 0.02),
        "conv_ln_g": 1.0 + nrm(ks[7], (DEPTH, D_GROUP), 0.02),
        "conv_ln_b": nrm(ks[8], (DEPTH, D_GROUP), 0.02),
        "s5_lambda_re": -0.5 + nrm(ks[9], (DEPTH, S5_GROUPS, S5_STATE), 0.01),
        "s5_lambda_im": lam_im_base + nrm(ks[10], (DEPTH, S5_GROUPS, S5_STATE), 0.01),
        "s5_b_re": nrm(ks[11], (DEPTH, S5_GROUPS, S5_STATE, S5_CH), (2 * S5_CH) ** -0.5),
        "s5_b_im": nrm(ks[12], (DEPTH, S5_GROUPS, S5_STATE, S5_CH), (2 * S5_CH) ** -0.5),
        "s5_c_re": nrm(ks[13], (DEPTH, S5_GROUPS, S5_CH, S5_STATE), S5_STATE ** -0.5),
        "s5_c_im": nrm(ks[14], (DEPTH, S5_GROUPS, S5_CH, S5_STATE), S5_STATE ** -0.5),
        "s5_d": nrm(ks[15], (DEPTH, D_GROUP), 0.5),
        "s5_log_dt": jax.random.uniform(ks[16], (DEPTH, S5_GROUPS), f32,
                                        math.log(S5_DT_MIN), math.log(S5_DT_MAX)),
        "s5_w_glu": nrm(ks[17], (DEPTH, D_GROUP, D_GROUP), D_GROUP ** -0.5),
        "attn_q_norm_g": 1.0 + nrm(ks[18], (DEPTH, ATT_DH), 0.02),
        "attn_k_norm_g": 1.0 + nrm(ks[19], (DEPTH, ATT_DH), 0.02),
        "rel_bias": nrm(ks[20], (REL_BUCKETS, ATT_HEADS), 0.5),
        "mix_out_norm_g": 1.0 + nrm(ks[21], (DEPTH, D_MIX), 0.02),
        "w_out": nrm(ks[22], (DEPTH, D_MIX, D_MODEL), D_MIX ** -0.5 * out_scale),
        "norm_mlp_g": 1.0 + nrm(ks[23], (DEPTH, D_MODEL), 0.02),
        "w_mlp_up": nrm(ks[24], (DEPTH, D_MODEL, D_FF), D_MODEL ** -0.5),
        "w_mlp_down": nrm(ks[25], (DEPTH, D_FF, D_MODEL), D_FF ** -0.5 * out_scale),
    }


def reference(x, norm_mix_g, w_in, hgrn_lb_logits, hgrn_norm_g, conv_w, conv_b, conv_ln_g,
              conv_ln_b, s5_lambda_re, s5_lambda_im, s5_b_re, s5_b_im, s5_c_re, s5_c_im, s5_d,
              s5_log_dt, s5_w_glu, attn_q_norm_g, attn_k_norm_g, rel_bias, mix_out_norm_g, w_out,
              norm_mlp_g, w_mlp_up, w_mlp_down):
    bsz, s, _ = x.shape
    lb_sm = jax.nn.softmax(hgrn_lb_logits.astype(jnp.float32), axis=0)
    lb_all = jnp.maximum(jnp.cumsum(lb_sm, axis=0) - lb_sm[0], 0.0)
    for l in range(DEPTH):
        h = rms_norm(x, norm_mix_g[l])
        z = h @ w_in[l]
        (a_q, a_f, a_i, a_g, b_val, b_gate, c_u, d_q, d_k, d_v) = jnp.split(z, N_IN_SLICES, axis=-1)
        y_a = hgrn2_mixer(a_q, a_f, a_i, a_g, lb_all[l], hgrn_norm_g[l])
        y_b = conformer_conv_mixer(b_val, b_gate, conv_w[l], conv_b[l], conv_ln_g[l], conv_ln_b[l])
        y_c = s5_mixer(c_u, s5_lambda_re[l], s5_lambda_im[l], s5_b_re[l], s5_b_im[l],
                       s5_c_re[l], s5_c_im[l], s5_d[l], s5_log_dt[l], s5_w_glu[l])
        y_d = dilated_attention_mixer(d_q, d_k, d_v, attn_q_norm_g[l], attn_k_norm_g[l], rel_bias)
        y = jnp.stack([y_a, y_b, y_c, y_d], axis=2)
        y = rms_norm(y, mix_out_norm_g[l].reshape(N_MIXERS, D_GROUP)).reshape(bsz, s, D_MIX)
        x = x + y @ w_out[l]
        hm = rms_norm(x, norm_mlp_g[l]) @ w_mlp_up[l]
        x = x + jnp.square(jax.nn.relu(hm)) @ w_mlp_down[l]
    return x
```

```python
import functools
import math

import numpy as np
import jax
import jax.numpy as jnp
from jax import lax
from jax.experimental import pallas as pl
from jax.experimental.pallas import tpu as pltpu

F32 = jnp.float32
BF16 = jnp.bfloat16

D_MODEL = 1024
D_GROUP = 256
N_MIXERS = 4
HG_HEADS = 4
HG_DK = 64
LB_FLOOR = 1e-30
CONV_WIDTH = 31
S5_CH = 16
S5_GROUPS = 16
S5_STATE = 64
ATT_HEADS = 4
ATT_DH = 64
ATT_PATTERNS = ((128, 1), (512, 4), (2048, 16))
ATT_BLOCK = 128
REL_BUCKETS = 32
REL_MAX_DIST = 2048
NEG_BIG = -1e30
D_FF = 4 * D_MODEL
N_IN_SLICES = 10
D_IN = N_IN_SLICES * D_GROUP
EPS = 1e-6

LANES = 128
SUBLANES = 8
VMEM_LIMIT = 56 * 1024 * 1024

ROW_TILE = 512
HG_CHUNK = 128
HG_BLOCK = 512
CONV_BLOCK = 512
CONV_SUB = 64
CONV_HALO = 32
S5_BLOCK = 256
S5_PITCH = S5_BLOCK + SUBLANES
S5_TILES = 2 * S5_GROUPS * S5_STATE // LANES
ATT_L = 2048


def _cparams(sem):
    return pltpu.CompilerParams(dimension_semantics=sem, vmem_limit_bytes=VMEM_LIMIT)


def _sigmoid(x):
    return 1.0 / (1.0 + jnp.exp(-x))


def _dot(a, b):
    return jnp.dot(a, b, preferred_element_type=F32)


def _dot_nt(a, b):
    return lax.dot_general(a, b, (((1,), (1,)), ((), ())), preferred_element_type=F32)


def _dot_tn(a, b):
    return lax.dot_general(a, b, (((0,), (0,)), ((), ())), preferred_element_type=F32)


def _inproj_kernel(x_ref, g_ref, w_ref, z_ref):
    x = x_ref[...]
    h = x * lax.rsqrt(jnp.mean(x * x, axis=-1, keepdims=True) + EPS) * g_ref[...]
    z_ref[...] = _dot(h.astype(BF16), w_ref[...])


def _inproj(x, g, w):
    bsz, s, d = x.shape
    n = w.shape[1]
    return pl.pallas_call(
        _inproj_kernel,
        grid=(bsz, s // ROW_TILE),
        in_specs=[
            pl.BlockSpec((None, ROW_TILE, d), lambda b, i: (b, i, 0)),
            pl.BlockSpec((1, d), lambda b, i: (0, 0)),
            pl.BlockSpec((d, n), lambda b, i: (0, 0)),
        ],
        out_specs=pl.BlockSpec((None, ROW_TILE, n), lambda b, i: (b, i, 0)),
        out_shape=jax.ShapeDtypeStruct((bsz, s, n), F32),
        compiler_params=_cparams(("parallel", "parallel")),
        name="inproj",
    )(x, g.reshape(1, d), w)


def _hold(gc, m):
    c, w = gc.shape
    if 2 * m >= SUBLANES:
        x = gc.reshape(c // (2 * m), 2 * m, w)
        return jnp.broadcast_to(x[:, m - 1:m, :], x.shape).reshape(c, w)
    x = gc.reshape(c // SUBLANES, SUBLANES, w)
    rows = lax.broadcasted_iota(jnp.int32, x.shape, 1)
    out = None
    for p in range(SUBLANES // (2 * m)):
        e = p * 2 * m + m - 1
        b = jnp.broadcast_to(x[:, e:e + 1, :], x.shape)
        out = b if out is None else jnp.where(rows >= p * 2 * m, b, out)
    return out.reshape(c, w)


def _hgrn_kernel(q_ref, f_ref, i_ref, g_ref, lb_ref, ng_ref, lvl_ref, o_ref, st_ref):
    @pl.when(pl.program_id(1) == 0)
    def _():
        st_ref[...] = jnp.zeros_like(st_ref)

    c = HG_CHUNK
    w = D_GROUP
    n_levels = int(math.log2(c))
    lane_head = lax.broadcasted_iota(jnp.int32, (c, w), 1) // HG_DK
    row = lax.broadcasted_iota(jnp.int32, (c, w), 0)
    blk = (lax.broadcasted_iota(jnp.int32, (w, w), 0) // HG_DK
           == lax.broadcasted_iota(jnp.int32, (w, w), 1) // HG_DK)
    ones_blk = jnp.where(blk, 1.0, 0.0).astype(BF16)
    lb = jnp.clip(lb_ref[...], 0.0, 1.0 - 1e-6)
    log_lb = jnp.log(jnp.maximum(lb, LB_FLOOR))
    log_1mlb = jnp.log1p(-lb)
    lvl = lvl_ref[...]

    def stack_heads(t):
        return jnp.concatenate(
            [jnp.where(lane_head == h, t, 0.0) for h in range(HG_HEADS)], axis=0).astype(BF16)

    def chunk(ci, carry):
        r0 = pl.multiple_of(ci * c, c)
        q = q_ref[pl.ds(r0, c), :]
        fz = f_ref[pl.ds(r0, c), :]
        v = i_ref[pl.ds(r0, c), :]
        gate = g_ref[pl.ds(r0, c), :]
        log_sig = jnp.minimum(fz, 0.0) - jnp.log1p(jnp.exp(-jnp.abs(fz)))
        b = log_1mlb + log_sig
        lf = jnp.maximum(log_lb, b) + jnp.log1p(jnp.exp(-jnp.abs(log_lb - b)))
        kk = (1.0 - lb) / (1.0 + jnp.exp(fz))
        gc = lf
        for sft in [2 ** j for j in range(n_levels)]:
            gc = gc + jnp.where(row >= sft, pltpu.roll(gc, sft, 0), 0.0)

        scores = jnp.zeros((c, HG_HEADS * c), F32)
        for lv in range(n_levels):
            e = jnp.exp(-jnp.abs(gc - _hold(gc, 2 ** lv)))
            p_lv = _dot_nt((q * e).astype(BF16), stack_heads(kk * e))
            scores = jnp.where(lvl == lv, p_lv, scores)
        o = _dot(scores.astype(BF16), stack_heads(v))
        o = o + _dot((q * kk).astype(BF16), ones_blk) * v
        st = st_ref[...]
        o = o + _dot_nt((q * jnp.exp(gc)).astype(BF16), st.astype(BF16))
        g_last = gc[c - 1:c, :]
        upd = _dot_tn(v.astype(BF16), (kk * jnp.exp(g_last - gc)).astype(BF16))
        st_ref[...] = st * jnp.exp(g_last) + jnp.where(blk, upd, 0.0)

        ms = _dot((o * o).astype(BF16), ones_blk) * (1.0 / HG_DK)
        y = o * lax.rsqrt(ms + EPS) * ng_ref[...]
        o_ref[pl.ds(r0, c), :] = y * (gate * _sigmoid(gate))
        return carry

    lax.fori_loop(0, q_ref.shape[0] // c, chunk, 0)


def _hgrn_levels():
    t = np.arange(HG_CHUNK)[:, None]
    s = np.arange(HG_CHUNK)[None, :]
    x = np.maximum(t ^ s, 1)
    lv = np.where(s < t, np.floor(np.log2(x)).astype(np.int32), -1).astype(np.int32)
    return np.tile(lv, (1, HG_HEADS))


def _hgrn(z, lb, norm_g):
    bsz, s, _ = z.shape
    tb = min(HG_BLOCK, s)

    def zspec(j):
        return pl.BlockSpec((None, tb, D_GROUP), lambda b, i, j=j: (b, i, j))

    const = lambda shape: pl.BlockSpec(shape, lambda b, i: (0, 0))
    return pl.pallas_call(
        _hgrn_kernel,
        grid=(bsz, s // tb),
        in_specs=[zspec(0), zspec(1), zspec(2), zspec(3),
                  const((1, D_GROUP)), const((1, D_GROUP)), const((HG_CHUNK, HG_HEADS * HG_CHUNK))],
        out_specs=pl.BlockSpec((None, tb, D_GROUP), lambda b, i: (b, i, 0)),
        out_shape=jax.ShapeDtypeStruct((bsz, s, D_GROUP), F32),
        scratch_shapes=[pltpu.VMEM((D_GROUP, D_GROUP), F32)],
        compiler_params=_cparams(("parallel", "arbitrary")),
        name="hgrn2",
    )(z, z, z, z, lb.reshape(1, D_GROUP), norm_g.reshape(1, D_GROUP), jnp.asarray(_hgrn_levels()))


def _conv_kernel(val_ref, gate_ref, w_ref, b_ref, lg_ref, lbias_ref, o_ref, h_ref):
    tb = val_ref.shape[0]

    @pl.when(pl.program_id(1) == 0)
    def _():
        h_ref[0:CONV_HALO, :] = jnp.zeros((CONV_HALO, D_GROUP), F32)

    @pl.when(pl.program_id(1) > 0)
    def _():
        h_ref[0:CONV_HALO, :] = h_ref[tb:tb + CONV_HALO, :]

    h_ref[CONV_HALO:CONV_HALO + tb, :] = val_ref[...] * _sigmoid(gate_ref[...])
    base = CONV_HALO - (CONV_WIDTH - 1)
    for t0 in range(0, tb, CONV_SUB):
        acc = jnp.broadcast_to(b_ref[...], (CONV_SUB, D_GROUP))
        for j in range(CONV_WIDTH):
            acc = acc + w_ref[j:j + 1, :] * h_ref[t0 + base + j:t0 + base + j + CONV_SUB, :]
        mu = jnp.mean(acc, axis=-1, keepdims=True)
        d = acc - mu
        var = jnp.mean(d * d, axis=-1, keepdims=True)
        y = d * lax.rsqrt(var + EPS) * lg_ref[...] + lbias_ref[...]
        o_ref[t0:t0 + CONV_SUB, :] = y * _sigmoid(y)


def _conv(z, conv_w, conv_b, ln_g, ln_b):
    bsz, s, _ = z.shape
    tb = min(CONV_BLOCK, s)
    const = lambda shape: pl.BlockSpec(shape, lambda b, i: (0, 0))
    return pl.pallas_call(
        _conv_kernel,
        grid=(bsz, s // tb),
        in_specs=[pl.BlockSpec((None, tb, D_GROUP), lambda b, i: (b, i, 4)),
                  pl.BlockSpec((None, tb, D_GROUP), lambda b, i: (b, i, 5)),
                  const((CONV_WIDTH, D_GROUP)), const((1, D_GROUP)), const((1, D_GROUP)),
                  const((1, D_GROUP))],
        out_specs=pl.BlockSpec((None, tb, D_GROUP), lambda b, i: (b, i, 0)),
        out_shape=jax.ShapeDtypeStruct((bsz, s, D_GROUP), F32),
        scratch_shapes=[pltpu.VMEM((CONV_HALO + tb, D_GROUP), F32)],
        compiler_params=_cparams(("parallel", "arbitrary")),
        name="conformer_conv",
    )(z, z, conv_w, conv_b.reshape(1, D_GROUP), ln_g.reshape(1, D_GROUP), ln_b.reshape(1, D_GROUP))


def _s5_kernel(u_ref, a_ref, bm_ref, cm_ref, d_ref, wg_ref, o_ref, bu_ref, hs_ref, st_ref):
    nb, t_len, _ = u_ref.shape
    half = S5_TILES // 2

    @pl.when(pl.program_id(0) == 0)
    def _():
        st_ref[...] = jnp.zeros_like(st_ref)

    for b in range(nb):
        bu = _dot(u_ref[b].astype(BF16), bm_ref[...])
        for i in range(S5_TILES):
            bu_ref[b, i * S5_PITCH:i * S5_PITCH + t_len, :] = bu[:, i * LANES:(i + 1) * LANES]

    ar = a_ref[0:half, :]
    ai = a_ref[half:S5_TILES, :]

    def step(t, hs):
        out = []
        for b in range(nb):
            hr, hi = hs[2 * b], hs[2 * b + 1]
            br = bu_ref[b, pl.ds(t, half, stride=S5_PITCH), :]
            bi = bu_ref[b, pl.ds(half * S5_PITCH + t, half, stride=S5_PITCH), :]
            nr = ar * hr - ai * hi + br
            ni = ar * hi + ai * hr + bi
            hs_ref[b, pl.ds(t, half, stride=S5_PITCH), :] = nr
            hs_ref[b, pl.ds(half * S5_PITCH + t, half, stride=S5_PITCH), :] = ni
            out += [nr, ni]
        return tuple(out)

    init = []
    for b in range(nb):
        init += [st_ref[b, 0:half, :], st_ref[b, half:S5_TILES, :]]
    hs = lax.fori_loop(0, t_len, step, tuple(init), unroll=8)
    for b in range(nb):
        st_ref[b, 0:half, :] = hs[2 * b]
        st_ref[b, half:S5_TILES, :] = hs[2 * b + 1]

    for b in range(nb):
        u = u_ref[b]
        hcat = jnp.concatenate(
            [hs_ref[b, i * S5_PITCH:i * S5_PITCH + t_len, :] for i in range(S5_TILES)], axis=1)
        y = _dot(hcat.astype(BF16), cm_ref[...]) + d_ref[...] * u
        y = 0.5 * y * (1.0 + jnp.tanh(math.sqrt(2.0 / math.pi) * (y + 0.044715 * (y * y * y))))
        o_ref[b] = y * _sigmoid(_dot(y.astype(BF16), wg_ref[...]))


def _s5_matrices(lam_re, lam_im, b_re, b_im, c_re, c_im, log_dt):
    dt = jnp.exp(log_dt.astype(F32))[:, None]
    lr = lam_re.astype(F32)
    li = lam_im.astype(F32)
    mag = jnp.exp(lr * dt)
    a_re = mag * jnp.cos(li * dt)
    a_im = mag * jnp.sin(li * dt)
    den = lr * lr + li * li
    coef_re = ((a_re - 1.0) * lr + a_im * li) / den
    coef_im = (a_im * lr - (a_re - 1.0) * li) / den
    br = b_re.astype(F32)
    bi = b_im.astype(F32)
    bbar_re = coef_re[..., None] * br - coef_im[..., None] * bi
    bbar_im = coef_re[..., None] * bi + coef_im[..., None] * br
    eye = jnp.eye(S5_GROUPS, dtype=F32)
    n_state = S5_GROUPS * S5_STATE

    def b_block(bb):
        return jnp.einsum('gpc,gh->gchp', bb, eye).reshape(D_GROUP, n_state)

    def c_block(cc):
        return jnp.einsum('gcp,gh->gphc', cc, eye).reshape(n_state, D_GROUP)

    bm = jnp.concatenate([b_block(bbar_re), b_block(bbar_im)], axis=1).astype(BF16)
    cm = jnp.concatenate([c_block(c_re.astype(F32)), -c_block(c_im.astype(F32))], axis=0).astype(BF16)
    a = jnp.concatenate([a_re.reshape(n_state // LANES, LANES), a_im.reshape(n_state // LANES, LANES)], 0)
    return a, bm, cm


def _s5(z, a, bm, cm, d_skip, w_glu):
    bsz, s, _ = z.shape
    tb = min(S5_BLOCK, s)
    assert tb == S5_BLOCK
    const = lambda shape: pl.BlockSpec(shape, lambda i: (0, 0))
    n_state2 = 2 * S5_GROUPS * S5_STATE
    return pl.pallas_call(
        _s5_kernel,
        grid=(s // tb,),
        in_specs=[pl.BlockSpec((bsz, tb, D_GROUP), lambda i: (0, i, 6)),
                  const((S5_TILES, LANES)), const((D_GROUP, n_state2)), const((n_state2, D_GROUP)),
                  const((1, D_GROUP)), const((D_GROUP, D_GROUP))],
        out_specs=pl.BlockSpec((bsz, tb, D_GROUP), lambda i: (0, i, 0)),
        out_shape=jax.ShapeDtypeStruct((bsz, s, D_GROUP), F32),
        scratch_shapes=[pltpu.VMEM((bsz, S5_TILES * S5_PITCH, LANES), F32),
                        pltpu.VMEM((bsz, S5_TILES * S5_PITCH, LANES), F32),
                        pltpu.VMEM((bsz, S5_TILES, LANES), F32)],
        compiler_params=_cparams(("arbitrary",)),
        name="s5",
    )(z, a, bm, cm, d_skip.reshape(1, D_GROUP), w_glu)


def _t5_bucket_np(dist):
    max_exact = REL_BUCKETS // 2
    distf = np.maximum(dist, max_exact).astype(np.float32)
    large = max_exact + (np.log(distf / np.float32(max_exact)) / np.float32(math.log(REL_MAX_DIST / max_exact))
                         * np.float32(REL_BUCKETS - max_exact)).astype(np.int32)
    large = np.minimum(large, REL_BUCKETS - 1)
    return np.where(dist < max_exact, dist, large)


def _attn_bias(rel_bias):
    i = np.arange(ATT_BLOCK)[:, None]
    j = np.arange(2 * ATT_BLOCK)[None, :]
    delta = i + ATT_BLOCK - j
    tabs = []
    for window, dilation in ATT_PATTERNS:
        span = window // dilation
        valid = (delta >= 0) & (delta <= span)
        bucket = _t5_bucket_np(np.maximum(delta, 0) * dilation)
        bias = jnp.transpose(rel_bias.astype(F32)[bucket], (2, 0, 1))
        tabs.append(jnp.where(valid[None], bias, NEG_BIG))
    return jnp.stack(tabs, axis=0)


def _attn_kernel(q_ref, k_ref, v_ref, qg_ref, kg_ref, bias_ref, o_ref,
                 qn_ref, kn_ref, vn_ref, num_ref, m_ref, l_ref):
    L = q_ref.shape[0]
    blk = ATT_BLOCK
    first = pl.program_id(1) == 0
    half_lane = lax.broadcasted_iota(jnp.int32, (blk, LANES), 1) < ATT_DH
    ones_blk = jnp.where(
        lax.broadcasted_iota(jnp.int32, (LANES, LANES), 0) // ATT_DH
        == lax.broadcasted_iota(jnp.int32, (LANES, LANES), 1) // ATT_DH, 1.0, 0.0).astype(BF16)

    @pl.when(first)
    def _():
        kn_ref[:, 0:L, :] = jnp.zeros((2, L, LANES), F32)
        vn_ref[:, 0:L, :] = jnp.zeros((2, L, LANES), F32)

    @pl.when(jnp.logical_not(first))
    def _():
        kn_ref[:, 0:L, :] = kn_ref[:, L:2 * L, :]
        vn_ref[:, 0:L, :] = vn_ref[:, L:2 * L, :]

    def head_norm(x, g):
        ms = _dot((x * x).astype(BF16), ones_blk) * (1.0 / ATT_DH)
        return x * lax.rsqrt(ms + EPS) * g

    for p in range(2):
        ls = slice(p * LANES, (p + 1) * LANES)
        qn_ref[p] = head_norm(q_ref[:, ls], qg_ref[...]) * (ATT_DH ** -0.5)
        kn_ref[p, L:2 * L, :] = head_norm(k_ref[:, ls], kg_ref[...])
        vn_ref[p, L:2 * L, :] = v_ref[:, ls]

    prev_cols = lax.broadcasted_iota(jnp.int32, (1, 2 * blk), 1) < blk
    pen_first = jnp.where(first, NEG_BIG, 0.0)

    for pi, (window, dil) in enumerate(ATT_PATTERNS):
        n_q = L // (dil * blk)

        def combo(ci, carry, pi=pi, dil=dil, n_q=n_q):
            r = ci % dil
            nb = ci // dil
            q0 = r + dil * blk * nb
            k0 = L + q0 - dil * blk
            pen = jnp.where(prev_cols, jnp.where(nb == 0, pen_first, 0.0), 0.0)
            for p in range(2):
                qb = qn_ref[p, pl.ds(q0, blk, stride=dil), :]
                kb = kn_ref[p, pl.ds(k0, 2 * blk, stride=dil), :].astype(BF16)
                vb = vn_ref[p, pl.ds(k0, 2 * blk, stride=dil), :].astype(BF16)
                o_h, m_h, l_h = [], [], []
                for hh in range(2):
                    sel = half_lane if hh == 0 else jnp.logical_not(half_lane)
                    s_ = _dot_nt(jnp.where(sel, qb, 0.0).astype(BF16), kb)
                    s_ = s_ + bias_ref[pi, 2 * p + hh] + pen
                    mx = jnp.max(s_, axis=-1, keepdims=True)
                    e = jnp.exp(s_ - mx)
                    l_h.append(jnp.sum(e, axis=-1, keepdims=True))
                    m_h.append(mx)
                    o_h.append(_dot(e.astype(BF16), vb))
                o_c = jnp.where(half_lane, o_h[0], o_h[1])
                m_c = jnp.where(half_lane, m_h[0], m_h[1])
                l_c = jnp.where(half_lane, l_h[0], l_h[1])
                if pi == 0:
                    num_ref[p, pl.ds(q0, blk, stride=dil), :] = o_c
                    m_ref[p, pl.ds(q0, blk, stride=dil), :] = m_c
                    l_ref[p, pl.ds(q0, blk, stride=dil), :] = l_c
                else:
                    m_o = m_ref[p, pl.ds(q0, blk, stride=dil), :]
                    m_n = jnp.maximum(m_o, m_c)
                    a_o = jnp.exp(m_o - m_n)
                    a_c = jnp.exp(m_c - m_n)
                    num_ref[p, pl.ds(q0, blk, stride=dil), :] = (
                        num_ref[p, pl.ds(q0, blk, stride=dil), :] * a_o + o_c * a_c)
                    l_ref[p, pl.ds(q0, blk, stride=dil), :] = (
                        l_ref[p, pl.ds(q0, blk, stride=dil), :] * a_o + l_c * a_c)
                    m_ref[p, pl.ds(q0, blk, stride=dil), :] = m_n
            return carry

        lax.fori_loop(0, dil * n_q, combo, 0)

    for p in range(2):
        o_ref[:, p * LANES:(p + 1) * LANES] = num_ref[p] / l_ref[p]


def _attn(z, q_g, k_g, bias):
    bsz, s, _ = z.shape
    L = ATT_L
    assert s % L == 0
    gq = jnp.concatenate([q_g, q_g]).reshape(1, LANES).astype(F32)
    gk = jnp.concatenate([k_g, k_g]).reshape(1, LANES).astype(F32)

    def zspec(j):
        return pl.BlockSpec((None, L, D_GROUP), lambda b, i, j=j: (b, i, j))

    return pl.pallas_call(
        _attn_kernel,
        grid=(bsz, s // L),
        in_specs=[zspec(7), zspec(8), zspec(9),
                  pl.BlockSpec((1, LANES), lambda b, i: (0, 0)),
                  pl.BlockSpec((1, LANES), lambda b, i: (0, 0)),
                  pl.BlockSpec(bias.shape, lambda b, i: (0, 0, 0, 0))],
        out_specs=pl.BlockSpec((None, L, D_GROUP), lambda b, i: (b, i, 0)),
        out_shape=jax.ShapeDtypeStruct((bsz, s, D_GROUP), F32),
        scratch_shapes=[pltpu.VMEM((2, L, LANES), F32),
                        pltpu.VMEM((2, 2 * L, LANES), F32),
                        pltpu.VMEM((2, 2 * L, LANES), F32),
                        pltpu.VMEM((2, L, LANES), F32),
                        pltpu.VMEM((2, L, LANES), F32),
                        pltpu.VMEM((2, L, LANES), F32)],
        compiler_params=_cparams(("parallel", "arbitrary")),
        name="dilated_attn",
    )(z, z, z, gq, gk, bias)


def _post_kernel(x_ref, ya_ref, yb_ref, yc_ref, yd_ref, gmix_ref, wout_ref, gmlp_ref, wup_ref, wdn_ref,
                 o_ref):
    acc = x_ref[...]
    for j, y_ref in enumerate((ya_ref, yb_ref, yc_ref, yd_ref)):
        y = y_ref[...]
        cs = slice(j * D_GROUP, (j + 1) * D_GROUP)
        yn = y * lax.rsqrt(jnp.mean(y * y, axis=-1, keepdims=True) + EPS) * gmix_ref[:, cs]
        acc = acc + _dot(yn.astype(BF16), wout_ref[cs, :])
    h = (acc * lax.rsqrt(jnp.mean(acc * acc, axis=-1, keepdims=True) + EPS) * gmlp_ref[...]).astype(BF16)
    mlp = jnp.zeros_like(acc)
    ff_tile = D_MODEL
    for f0 in range(0, D_FF, ff_tile):
        hm = jnp.maximum(_dot(h, wup_ref[:, f0:f0 + ff_tile]), 0.0)
        mlp = mlp + _dot((hm * hm).astype(BF16), wdn_ref[f0:f0 + ff_tile, :])
    o_ref[...] = acc + mlp


def _post(x, ys, gmix, wout, gmlp, wup, wdn):
    bsz, s, d = x.shape
    row = lambda n: pl.BlockSpec((None, ROW_TILE, n), lambda b, i: (b, i, 0))
    const = lambda shape: pl.BlockSpec(shape, lambda b, i: (0, 0), pipeline_mode=pl.Buffered(1))
    return pl.pallas_call(
        _post_kernel,
        grid=(bsz, s // ROW_TILE),
        in_specs=[row(d), row(D_GROUP), row(D_GROUP), row(D_GROUP), row(D_GROUP),
                  const((1, d)), const((d, d)), const((1, d)), const((d, D_FF)), const((D_FF, d))],
        out_specs=row(d),
        out_shape=jax.ShapeDtypeStruct((bsz, s, d), F32),
        compiler_params=_cparams(("parallel", "parallel")),
        name="outproj_mlp",
    )(x, *ys, gmix.reshape(1, d), wout, gmlp.reshape(1, d), wup, wdn)


def kernel(x, norm_mix_g, w_in, hgrn_lb_logits, hgrn_norm_g, conv_w, conv_b, conv_ln_g, conv_ln_b, s5_lambda_re, s5_lambda_im, s5_b_re, s5_b_im, s5_c_re, s5_c_im, s5_d, s5_log_dt, s5_w_glu, attn_q_norm_g, attn_k_norm_g, rel_bias, mix_out_norm_g, w_out, norm_mlp_g, w_mlp_up, w_mlp_down):
    depth = w_in.shape[0]
    lb_sm = jax.nn.softmax(hgrn_lb_logits.astype(F32), axis=0)
    lb_all = jnp.maximum(jnp.cumsum(lb_sm, axis=0) - lb_sm[0], 0.0)
    bias = _attn_bias(rel_bias)
    for l in range(depth):
        z = _inproj(x, norm_mix_g[l], w_in[l].astype(BF16))
        y_a = _hgrn(z, lb_all[l], hgrn_norm_g[l])
        y_b = _conv(z, conv_w[l], conv_b[l], conv_ln_g[l], conv_ln_b[l])
        a, bm, cm = _s5_matrices(s5_lambda_re[l], s5_lambda_im[l], s5_b_re[l], s5_b_im[l],
                                 s5_c_re[l], s5_c_im[l], s5_log_dt[l])
        y_c = _s5(z, a, bm, cm, s5_d[l], s5_w_glu[l].astype(BF16))
        y_d = _attn(z, attn_q_norm_g[l], attn_k_norm_g[l], bias)
        x = _post(x, (y_a, y_b, y_c, y_d), mix_out_norm_g[l], w_out[l].astype(BF16), norm_mlp_g[l],
                  w_mlp_up[l].astype(BF16), w_mlp_down[l].astype(BF16))
    return x
```

```python
import functools
import math

import numpy as np
import jax
import jax.numpy as jnp
from jax import lax
from jax.experimental import pallas as pl
from jax.experimental.pallas import tpu as pltpu

F32 = jnp.float32
BF16 = jnp.bfloat16

D_MODEL = 1024
D_GROUP = 256
N_MIXERS = 4
HG_HEADS = 4
HG_DK = 64
LB_FLOOR = 1e-30
CONV_WIDTH = 31
S5_CH = 16
S5_GROUPS = 16
S5_STATE = 64
ATT_HEADS = 4
ATT_DH = 64
ATT_PATTERNS = ((128, 1), (512, 4), (2048, 16))
ATT_BLOCK = 128
REL_BUCKETS = 32
REL_MAX_DIST = 2048
NEG_BIG = -1e30
D_FF = 4 * D_MODEL
N_IN_SLICES = 10
D_IN = N_IN_SLICES * D_GROUP
EPS = 1e-6
LOG2E = 1.4426950408889634

LANES = 128
SUBLANES = 8
VMEM_LIMIT = 56 * 1024 * 1024

ROW_TILE = 512
HG_CHUNK = 128
HG_BLOCK = 512
CONV_BLOCK = 512
CONV_SUB = 64
CONV_HALO = 32
S5_BLOCK = 256
S5_PITCH = S5_BLOCK + SUBLANES
S5_TILES = 2 * S5_GROUPS * S5_STATE // LANES
ATT_L = 2048


def _cparams(sem):
    return pltpu.CompilerParams(dimension_semantics=sem, vmem_limit_bytes=VMEM_LIMIT)


def _sigmoid(x):
    return 1.0 / (1.0 + jnp.exp(-x))


def _dot(a, b):
    return jnp.dot(a, b, preferred_element_type=F32)


def _dot_nt(a, b):
    return lax.dot_general(a, b, (((1,), (1,)), ((), ())), preferred_element_type=F32)


def _dot_tn(a, b):
    return lax.dot_general(a, b, (((0,), (0,)), ((), ())), preferred_element_type=F32)


def _inproj_kernel(x_ref, g_ref, w_ref, z_ref):
    x = x_ref[...]
    h = x * lax.rsqrt(jnp.mean(x * x, axis=-1, keepdims=True) + EPS) * g_ref[...]
    z_ref[...] = _dot(h.astype(BF16), w_ref[...])


def _inproj(x, g, w):
    bsz, s, d = x.shape
    n = w.shape[1]
    return pl.pallas_call(
        _inproj_kernel,
        grid=(bsz, s // ROW_TILE),
        in_specs=[
            pl.BlockSpec((None, ROW_TILE, d), lambda b, i: (b, i, 0)),
            pl.BlockSpec((1, d), lambda b, i: (0, 0)),
            pl.BlockSpec((d, n), lambda b, i: (0, 0)),
        ],
        out_specs=pl.BlockSpec((None, ROW_TILE, n), lambda b, i: (b, i, 0)),
        out_shape=jax.ShapeDtypeStruct((bsz, s, n), F32),
        compiler_params=_cparams(("parallel", "parallel")),
        name="inproj",
    )(x, g.reshape(1, d), w)


def _hold(gc, m):
    c, w = gc.shape
    if 2 * m >= SUBLANES:
        x = gc.reshape(c // (2 * m), 2 * m, w)
        return jnp.broadcast_to(x[:, m - 1:m, :], x.shape).reshape(c, w)
    x = gc.reshape(c // SUBLANES, SUBLANES, w)
    rows = lax.broadcasted_iota(jnp.int32, x.shape, 1)
    out = None
    for p in range(SUBLANES // (2 * m)):
        e = p * 2 * m + m - 1
        b = jnp.broadcast_to(x[:, e:e + 1, :], x.shape)
        out = b if out is None else jnp.where(rows >= p * 2 * m, b, out)
    return out.reshape(c, w)


def _hgrn_kernel(q_ref, f_ref, i_ref, g_ref, lb_ref, ng_ref, lvl_ref, o_ref, st_ref):
    @pl.when(pl.program_id(1) == 0)
    def _():
        st_ref[...] = jnp.zeros_like(st_ref)

    c = HG_CHUNK
    w = D_GROUP
    n_levels = int(math.log2(c))
    lane_head = lax.broadcasted_iota(jnp.int32, (c, w), 1) // HG_DK
    row = lax.broadcasted_iota(jnp.int32, (c, w), 0)
    blk = (lax.broadcasted_iota(jnp.int32, (w, w), 0) // HG_DK
           == lax.broadcasted_iota(jnp.int32, (w, w), 1) // HG_DK)
    ones_blk = jnp.where(blk, 1.0, 0.0).astype(BF16)
    lb = jnp.clip(lb_ref[...], 0.0, 1.0 - 1e-6)
    lb_floor = jnp.maximum(lb, LB_FLOOR)
    tri = jnp.where(lax.broadcasted_iota(jnp.int32, (c, c), 0) >= lax.broadcasted_iota(jnp.int32, (c, c), 1),
                    1.0, 0.0).astype(BF16)
    sgn = [jnp.where((row & (2 ** lv)) != 0, 1.0, -1.0) for lv in range(n_levels)]
    grp = SUBLANES

    def stack_heads(t):
        return jnp.concatenate(
            [jnp.where(lane_head == h, t, 0.0).astype(BF16) for h in range(HG_HEADS)], axis=0)

    def chunk(r0, st):
        q = q_ref[r0:r0 + c, :]
        fz = f_ref[r0:r0 + c, :]
        v = i_ref[r0:r0 + c, :]
        gate = g_ref[r0:r0 + c, :]
        gated = (1.0 - lb) * _sigmoid(fz)
        lf = jnp.log2(lb_floor + gated)
        kk = (1.0 - lb) - gated
        hi = lf.astype(BF16)
        r1 = lf - hi.astype(F32)
        mid = r1.astype(BF16)
        lo = (r1 - mid.astype(F32)).astype(BF16)
        gc = _dot(tri, hi) + _dot(tri, mid) + _dot(tri, lo)

        q_b = q.astype(BF16)
        k_b = [jnp.where(lane_head == h, kk, 0.0).astype(BF16) for h in range(HG_HEADS)]
        n_grp = c // grp
        scores = [None] * n_grp
        for lv in range(n_levels):
            m = 2 ** lv
            e = jnp.exp2((gc - _hold(gc, m)) * sgn[lv]).astype(BF16)
            p_lv = _dot_nt(q_b * e, jnp.concatenate([kb * e for kb in k_b], axis=0))
            for gi in range(n_grp):
                if m >= grp and (gi * grp) & m == 0:
                    continue
                rs = slice(gi * grp, (gi + 1) * grp)
                old = 0.0 if scores[gi] is None else scores[gi]
                scores[gi] = jnp.where(lvl_ref[rs, :] == lv, p_lv[rs], old)
        o = _dot(jnp.concatenate(scores, axis=0).astype(BF16), stack_heads(v))
        o = o + _dot((q * kk).astype(BF16), ones_blk) * v
        o = o + _dot_nt((q * jnp.exp2(gc)).astype(BF16), st.astype(BF16))
        g_last = gc[c - 1:c, :]
        upd = _dot_tn(v.astype(BF16), (kk * jnp.exp2(g_last - gc)).astype(BF16))
        st = st * jnp.exp2(g_last) + jnp.where(blk, upd, 0.0)

        ms = _dot((o * o).astype(BF16), ones_blk) * (1.0 / HG_DK)
        y = o * lax.rsqrt(ms + EPS) * ng_ref[...]
        o_ref[r0:r0 + c, :] = y * (gate * _sigmoid(gate))
        return st

    st = st_ref[...]
    for r0 in range(0, q_ref.shape[0], c):
        st = chunk(r0, st)
    st_ref[...] = st


def _hgrn_levels():
    t = np.arange(HG_CHUNK)[:, None]
    s = np.arange(HG_CHUNK)[None, :]
    x = np.maximum(t ^ s, 1)
    lv = np.where(s < t, np.floor(np.log2(x)).astype(np.int32), -1).astype(np.int32)
    return np.tile(lv, (1, HG_HEADS))


def _hgrn(z, lb, norm_g):
    bsz, s, _ = z.shape
    tb = min(HG_BLOCK, s)

    def zspec(j):
        return pl.BlockSpec((None, tb, D_GROUP), lambda b, i, j=j: (b, i, j))

    const = lambda shape: pl.BlockSpec(shape, lambda b, i: (0, 0))
    return pl.pallas_call(
        _hgrn_kernel,
        grid=(bsz, s // tb),
        in_specs=[zspec(0), zspec(1), zspec(2), zspec(3),
                  const((1, D_GROUP)), const((1, D_GROUP)), const((HG_CHUNK, HG_HEADS * HG_CHUNK))],
        out_specs=pl.BlockSpec((None, tb, D_GROUP), lambda b, i: (b, i, 0)),
        out_shape=jax.ShapeDtypeStruct((bsz, s, D_GROUP), F32),
        scratch_shapes=[pltpu.VMEM((D_GROUP, D_GROUP), F32)],
        compiler_params=_cparams(("parallel", "arbitrary")),
        name="hgrn2",
    )(z, z, z, z, lb.reshape(1, D_GROUP), norm_g.reshape(1, D_GROUP), jnp.asarray(_hgrn_levels()))


def _conv_kernel(val_ref, gate_ref, w_ref, b_ref, lg_ref, lbias_ref, o_ref, h_ref):
    tb = val_ref.shape[0]

    @pl.when(pl.program_id(1) == 0)
    def _():
        h_ref[0:CONV_HALO, :] = jnp.zeros((CONV_HALO, D_GROUP), F32)

    @pl.when(pl.program_id(1) > 0)
    def _():
        h_ref[0:CONV_HALO, :] = h_ref[tb:tb + CONV_HALO, :]

    h_ref[CONV_HALO:CONV_HALO + tb, :] = val_ref[...] * _sigmoid(gate_ref[...])
    base = CONV_HALO - (CONV_WIDTH - 1)
    for t0 in range(0, tb, CONV_SUB):
        acc = jnp.broadcast_to(b_ref[...], (CONV_SUB, D_GROUP))
        for c in range(SUBLANES):
            part = None
            rows = CONV_SUB + (SUBLANES if c else 0)
            for j in range(CONV_WIDTH):
                if (base + j) % SUBLANES != c:
                    continue
                r0 = t0 + base + j - c
                term = w_ref[j:j + 1, :] * h_ref[r0:r0 + rows, :]
                part = term if part is None else part + term
            acc = acc + part[c:c + CONV_SUB, :]
        mu = jnp.mean(acc, axis=-1, keepdims=True)
        d = acc - mu
        var = jnp.mean(d * d, axis=-1, keepdims=True)
        y = d * lax.rsqrt(var + EPS) * lg_ref[...] + lbias_ref[...]
        o_ref[t0:t0 + CONV_SUB, :] = y * _sigmoid(y)


def _conv(z, conv_w, conv_b, ln_g, ln_b):
    bsz, s, _ = z.shape
    tb = min(CONV_BLOCK, s)
    const = lambda shape: pl.BlockSpec(shape, lambda b, i: (0, 0))
    return pl.pallas_call(
        _conv_kernel,
        grid=(bsz, s // tb),
        in_specs=[pl.BlockSpec((None, tb, D_GROUP), lambda b, i: (b, i, 4)),
                  pl.BlockSpec((None, tb, D_GROUP), lambda b, i: (b, i, 5)),
                  const((CONV_WIDTH, D_GROUP)), const((1, D_GROUP)), const((1, D_GROUP)),
                  const((1, D_GROUP))],
        out_specs=pl.BlockSpec((None, tb, D_GROUP), lambda b, i: (b, i, 0)),
        out_shape=jax.ShapeDtypeStruct((bsz, s, D_GROUP), F32),
        scratch_shapes=[pltpu.VMEM((CONV_HALO + tb, D_GROUP), F32)],
        compiler_params=_cparams(("parallel", "arbitrary")),
        name="conformer_conv",
    )(z, z, conv_w, conv_b.reshape(1, D_GROUP), ln_g.reshape(1, D_GROUP), ln_b.reshape(1, D_GROUP))


def _s5_kernel(u_ref, a_ref, bm_ref, cm_ref, d_ref, wg_ref, o_ref, bu_ref, hs_ref, st_ref):
    nb, t_len, _ = u_ref.shape
    half = S5_TILES // 2

    @pl.when(pl.program_id(0) == 0)
    def _():
        st_ref[...] = jnp.zeros_like(st_ref)

    for b in range(nb):
        bu = _dot(u_ref[b].astype(BF16), bm_ref[...])
        for i in range(S5_TILES):
            bu_ref[b, i * S5_PITCH:i * S5_PITCH + t_len, :] = bu[:, i * LANES:(i + 1) * LANES]

    ar = a_ref[0:half, :]
    ai = a_ref[half:S5_TILES, :]

    def step(t, hs):
        out = []
        for b in range(nb):
            hr, hi = hs[2 * b], hs[2 * b + 1]
            br = bu_ref[b, pl.ds(t, half, stride=S5_PITCH), :]
            bi = bu_ref[b, pl.ds(half * S5_PITCH + t, half, stride=S5_PITCH), :]
            nr = ar * hr - ai * hi + br
            ni = ar * hi + ai * hr + bi
            hs_ref[b, pl.ds(t, half, stride=S5_PITCH), :] = nr
            hs_ref[b, pl.ds(half * S5_PITCH + t, half, stride=S5_PITCH), :] = ni
            out += [nr, ni]
        return tuple(out)

    init = []
    for b in range(nb):
        init += [st_ref[b, 0:half, :], st_ref[b, half:S5_TILES, :]]
    hs = lax.fori_loop(0, t_len, step, tuple(init), unroll=8)
    for b in range(nb):
        st_ref[b, 0:half, :] = hs[2 * b]
        st_ref[b, half:S5_TILES, :] = hs[2 * b + 1]

    for b in range(nb):
        u = u_ref[b]
        hcat = jnp.concatenate(
            [hs_ref[b, i * S5_PITCH:i * S5_PITCH + t_len, :] for i in range(S5_TILES)], axis=1)
        y = _dot(hcat.astype(BF16), cm_ref[...]) + d_ref[...] * u
        y = 0.5 * y * (1.0 + jnp.tanh(math.sqrt(2.0 / math.pi) * (y + 0.044715 * (y * y * y))))
        o_ref[b] = y * _sigmoid(_dot(y.astype(BF16), wg_ref[...]))


def _s5_matrices(lam_re, lam_im, b_re, b_im, c_re, c_im, log_dt):
    dt = jnp.exp(log_dt.astype(F32))[:, None]
    lr = lam_re.astype(F32)
    li = lam_im.astype(F32)
    mag = jnp.exp(lr * dt)
    a_re = mag * jnp.cos(li * dt)
    a_im = mag * jnp.sin(li * dt)
    den = lr * lr + li * li
    coef_re = ((a_re - 1.0) * lr + a_im * li) / den
    coef_im = (a_im * lr - (a_re - 1.0) * li) / den
    br = b_re.astype(F32)
    bi = b_im.astype(F32)
    bbar_re = coef_re[..., None] * br - coef_im[..., None] * bi
    bbar_im = coef_re[..., None] * bi + coef_im[..., None] * br
    eye = jnp.eye(S5_GROUPS, dtype=F32)
    n_state = S5_GROUPS * S5_STATE

    def b_block(bb):
        return jnp.einsum('gpc,gh->gchp', bb, eye).reshape(D_GROUP, n_state)

    def c_block(cc):
        return jnp.einsum('gcp,gh->gphc', cc, eye).reshape(n_state, D_GROUP)

    bm = jnp.concatenate([b_block(bbar_re), b_block(bbar_im)], axis=1).astype(BF16)
    cm = jnp.concatenate([c_block(c_re.astype(F32)), -c_block(c_im.astype(F32))], axis=0).astype(BF16)
    a = jnp.concatenate([a_re.reshape(n_state // LANES, LANES), a_im.reshape(n_state // LANES, LANES)], 0)
    return a, bm, cm


def _s5(z, a, bm, cm, d_skip, w_glu):
    bsz, s, _ = z.shape
    tb = min(S5_BLOCK, s)
    assert tb == S5_BLOCK
    const = lambda shape: pl.BlockSpec(shape, lambda i: (0, 0))
    n_state2 = 2 * S5_GROUPS * S5_STATE
    return pl.pallas_call(
        _s5_kernel,
        grid=(s // tb,),
        in_specs=[pl.BlockSpec((bsz, tb, D_GROUP), lambda i: (0, i, 6)),
                  const((S5_TILES, LANES)), const((D_GROUP, n_state2)), const((n_state2, D_GROUP)),
                  const((1, D_GROUP)), const((D_GROUP, D_GROUP))],
        out_specs=pl.BlockSpec((bsz, tb, D_GROUP), lambda i: (0, i, 0)),
        out_shape=jax.ShapeDtypeStruct((bsz, s, D_GROUP), F32),
        scratch_shapes=[pltpu.VMEM((bsz, S5_TILES * S5_PITCH, LANES), F32),
                        pltpu.VMEM((bsz, S5_TILES * S5_PITCH, LANES), F32),
                        pltpu.VMEM((bsz, S5_TILES, LANES), F32)],
        compiler_params=_cparams(("arbitrary",)),
        name="s5",
    )(z, a, bm, cm, d_skip.reshape(1, D_GROUP), w_glu)


def _t5_bucket_np(dist):
    max_exact = REL_BUCKETS // 2
    distf = np.maximum(dist, max_exact).astype(np.float32)
    large = max_exact + (np.log(distf / np.float32(max_exact)) / np.float32(math.log(REL_MAX_DIST / max_exact))
                         * np.float32(REL_BUCKETS - max_exact)).astype(np.int32)
    large = np.minimum(large, REL_BUCKETS - 1)
    return np.where(dist < max_exact, dist, large)


def _attn_bias(rel_bias):
    blk = ATT_BLOCK
    period = 3 * blk + 1
    k = np.arange(-(blk - 1), 2 * blk)
    tabs = []
    for window, dilation in ATT_PATTERNS:
        delta = blk - k
        valid = (delta >= 0) & (delta <= window // dilation)
        bucket = _t5_bucket_np(np.maximum(delta, 0) * dilation)
        vals = jnp.where(valid[:, None], rel_bias.astype(F32)[bucket] * LOG2E, NEG_BIG)
        v = jnp.full((period, ATT_HEADS), NEG_BIG, F32).at[k % period].set(vals).T
        tiled = jnp.tile(v, (1, blk))[:, :blk * (period - 1)]
        tabs.append(tiled.reshape(ATT_HEADS, blk, period - 1)[:, :, :2 * blk])
    return jnp.concatenate(tabs, axis=0)


def _attn_kernel(q_ref, k_ref, v_ref, qg_ref, kg_ref, bias_ref, o_ref,
                 qn_ref, kn_ref, vn_ref, on_ref, lse_ref, btab_ref, sc0_ref, sc1_ref):
    L = q_ref.shape[0]
    blk = ATT_BLOCK
    n_pat = len(ATT_PATTERNS)
    step = pl.program_id(1)
    first = step == 0
    half_lane = lax.broadcasted_iota(jnp.int32, (blk, LANES), 1) < ATT_DH
    ones_blk = jnp.where(
        lax.broadcasted_iota(jnp.int32, (LANES, LANES), 0) // ATT_DH
        == lax.broadcasted_iota(jnp.int32, (LANES, LANES), 1) // ATT_DH, 1.0, 0.0).astype(BF16)

    @pl.when(first)
    def _():
        kn_ref[:, 0:L, :] = jnp.zeros((2, L, LANES), F32)
        vn_ref[:, 0:L, :] = jnp.zeros((2, L, LANES), F32)
        prev_cols = lax.broadcasted_iota(jnp.int32, (blk, 2 * blk), 1) < blk
        for t in range(n_pat * ATT_HEADS):
            btab_ref[0, t] = bias_ref[t]
            btab_ref[1, t] = jnp.where(prev_cols, NEG_BIG, bias_ref[t])

    @pl.when(step == 1)
    def _():
        for t in range(n_pat * ATT_HEADS):
            btab_ref[1, t] = bias_ref[t]

    @pl.when(jnp.logical_not(first))
    def _():
        kn_ref[:, 0:L, :] = kn_ref[:, L:2 * L, :]
        vn_ref[:, 0:L, :] = vn_ref[:, L:2 * L, :]

    def head_norm(x, g):
        ms = _dot((x * x).astype(BF16), ones_blk) * (1.0 / ATT_DH)
        return x * lax.rsqrt(ms + EPS) * g

    for p in range(2):
        ls = slice(p * LANES, (p + 1) * LANES)
        qn_ref[p] = head_norm(q_ref[:, ls], qg_ref[...]) * (ATT_DH ** -0.5 * LOG2E)
        kn_ref[p, L:2 * L, :] = head_norm(k_ref[:, ls], kg_ref[...])
        vn_ref[p, L:2 * L, :] = v_ref[:, ls]

    def rows(start, n, dil):
        return pl.ds(start, n) if dil == 1 else pl.ds(start, n, stride=dil)

    ones_cols = jnp.ones((2 * blk, LANES), BF16)
    n_combo = L // blk

    def offsets(ci, dil):
        q0 = (ci % dil) + dil * blk * (ci // dil)
        return q0, L + q0 - dil * blk

    def scores(pi, ci, s_ref):
        dil = ATT_PATTERNS[pi][1]
        q0, k0 = offsets(ci, dil)
        tsel = jnp.where(ci < dil, 1, 0)
        for p in range(2):
            qb = qn_ref[p, rows(q0, blk, dil), :]
            kb = kn_ref[p, rows(k0, 2 * blk, dil), :].astype(BF16)
            for hh in range(2):
                sel = half_lane if hh == 0 else jnp.logical_not(half_lane)
                h = 2 * p + hh
                s_ref[h] = (_dot_nt(jnp.where(sel, qb, 0.0).astype(BF16), kb)
                            + btab_ref[tsel, pi * ATT_HEADS + h])

    def softmax_pv(pi, ci, s_ref):
        dil = ATT_PATTERNS[pi][1]
        q0, k0 = offsets(ci, dil)
        for p in range(2):
            vb = jnp.concatenate([vn_ref[p, rows(k0, 2 * blk, dil), :].astype(BF16), ones_cols], axis=1)
            mxs, ovs = [], []
            for hh in range(2):
                sc = s_ref[2 * p + hh]
                mx = jnp.max(sc, axis=-1, keepdims=True)
                ovs.append(_dot(jnp.exp2(sc - mx).astype(BF16), vb))
                mxs.append(mx)
            den = jnp.where(half_lane, ovs[0][:, LANES:], ovs[1][:, LANES:])
            num = jnp.where(half_lane, ovs[0][:, :LANES], ovs[1][:, :LANES])
            on_ref[pi, p, rows(q0, blk, dil), :] = num * (1.0 / den)
            lse_ref[pi, p, rows(q0, blk, dil), :] = jnp.where(half_lane, mxs[0], mxs[1]) + jnp.log2(den)

    scores(0, 0, sc0_ref)
    for pi in range(n_pat):
        def pair(j, carry, pi=pi):
            scores(pi, 2 * j + 1, sc1_ref)
            softmax_pv(pi, 2 * j, sc0_ref)
            scores(pi, 2 * j + 2, sc0_ref)
            softmax_pv(pi, 2 * j + 1, sc1_ref)
            return carry

        lax.fori_loop(0, n_combo // 2 - 1, pair, 0)
        scores(pi, n_combo - 1, sc1_ref)
        softmax_pv(pi, n_combo - 2, sc0_ref)
        if pi + 1 < n_pat:
            scores(pi + 1, 0, sc0_ref)
        softmax_pv(pi, n_combo - 1, sc1_ref)

    def merge(ti, carry):
        t0 = pl.multiple_of(ti * blk, blk)
        for p in range(2):
            ls_ = [lse_ref[pi, p, pl.ds(t0, blk), :] for pi in range(n_pat)]
            mx = functools.reduce(jnp.maximum, ls_)
            ws = [jnp.exp2(x - mx) for x in ls_]
            num = sum(ws[pi] * on_ref[pi, p, pl.ds(t0, blk), :] for pi in range(n_pat))
            o_ref[pl.ds(t0, blk), p * LANES:(p + 1) * LANES] = num / sum(ws)
        return carry

    lax.fori_loop(0, L // blk, merge, 0)


def _attn(z, q_g, k_g, bias):
    bsz, s, _ = z.shape
    L = ATT_L
    assert s % L == 0
    gq = jnp.concatenate([q_g, q_g]).reshape(1, LANES).astype(F32)
    gk = jnp.concatenate([k_g, k_g]).reshape(1, LANES).astype(F32)

    def zspec(j):
        return pl.BlockSpec((None, L, D_GROUP), lambda b, i, j=j: (b, i, j))

    return pl.pallas_call(
        _attn_kernel,
        grid=(bsz, s // L),
        in_specs=[zspec(7), zspec(8), zspec(9),
                  pl.BlockSpec((1, LANES), lambda b, i: (0, 0)),
                  pl.BlockSpec((1, LANES), lambda b, i: (0, 0)),
                  pl.BlockSpec(bias.shape, lambda b, i: (0, 0, 0))],
        out_specs=pl.BlockSpec((None, L, D_GROUP), lambda b, i: (b, i, 0)),
        out_shape=jax.ShapeDtypeStruct((bsz, s, D_GROUP), F32),
        scratch_shapes=[pltpu.VMEM((2, L, LANES), F32),
                        pltpu.VMEM((2, 2 * L, LANES), F32),
                        pltpu.VMEM((2, 2 * L, LANES), F32),
                        pltpu.VMEM((len(ATT_PATTERNS), 2, L, LANES), F32),
                        pltpu.VMEM((len(ATT_PATTERNS), 2, L, LANES), F32),
                        pltpu.VMEM((2,) + bias.shape, F32),
                        pltpu.VMEM((ATT_HEADS, ATT_BLOCK, 2 * ATT_BLOCK), F32),
                        pltpu.VMEM((ATT_HEADS, ATT_BLOCK, 2 * ATT_BLOCK), F32)],
        compiler_params=_cparams(("parallel", "arbitrary")),
        name="dilated_attn",
    )(z, z, z, gq, gk, bias)


def _post_kernel(x_ref, ya_ref, yb_ref, yc_ref, yd_ref, gmix_ref, wout_ref, gmlp_ref, wup_ref, wdn_ref,
                 o_ref):
    acc = x_ref[...]
    for j, y_ref in enumerate((ya_ref, yb_ref, yc_ref, yd_ref)):
        y = y_ref[...]
        cs = slice(j * D_GROUP, (j + 1) * D_GROUP)
        yn = y * lax.rsqrt(jnp.mean(y * y, axis=-1, keepdims=True) + EPS) * gmix_ref[:, cs]
        acc = acc + _dot(yn.astype(BF16), wout_ref[cs, :])
    h = (acc * lax.rsqrt(jnp.mean(acc * acc, axis=-1, keepdims=True) + EPS) * gmlp_ref[...]).astype(BF16)
    mlp = jnp.zeros_like(acc)
    ff_tile = D_MODEL
    for f0 in range(0, D_FF, ff_tile):
        hm = jnp.maximum(_dot(h, wup_ref[:, f0:f0 + ff_tile]), 0.0)
        mlp = mlp + _dot((hm * hm).astype(BF16), wdn_ref[f0:f0 + ff_tile, :])
    o_ref[...] = acc + mlp


def _post(x, ys, gmix, wout, gmlp, wup, wdn):
    bsz, s, d = x.shape
    row = lambda n: pl.BlockSpec((None, ROW_TILE, n), lambda b, i: (b, i, 0))
    const = lambda shape: pl.BlockSpec(shape, lambda b, i: (0, 0), pipeline_mode=pl.Buffered(1))
    return pl.pallas_call(
        _post_kernel,
        grid=(bsz, s // ROW_TILE),
        in_specs=[row(d), row(D_GROUP), row(D_GROUP), row(D_GROUP), row(D_GROUP),
                  const((1, d)), const((d, d)), const((1, d)), const((d, D_FF)), const((D_FF, d))],
        out_specs=row(d),
        out_shape=jax.ShapeDtypeStruct((bsz, s, d), F32),
        compiler_params=_cparams(("parallel", "parallel")),
        name="outproj_mlp",
    )(x, *ys, gmix.reshape(1, d), wout, gmlp.reshape(1, d), wup, wdn)


def kernel(x, norm_mix_g, w_in, hgrn_lb_logits, hgrn_norm_g, conv_w, conv_b, conv_ln_g, conv_ln_b, s5_lambda_re, s5_lambda_im, s5_b_re, s5_b_im, s5_c_re, s5_c_im, s5_d, s5_log_dt, s5_w_glu, attn_q_norm_g, attn_k_norm_g, rel_bias, mix_out_norm_g, w_out, norm_mlp_g, w_mlp_up, w_mlp_down):
    depth = w_in.shape[0]
    lb_sm = jax.nn.softmax(hgrn_lb_logits.astype(F32), axis=0)
    lb_all = jnp.maximum(jnp.cumsum(lb_sm, axis=0) - lb_sm[0], 0.0)
    bias = _attn_bias(rel_bias)
    for l in range(depth):
        z = _inproj(x, norm_mix_g[l], w_in[l].astype(BF16))
        y_a = _hgrn(z, lb_all[l], hgrn_norm_g[l])
        y_b = _conv(z, conv_w[l], conv_b[l], conv_ln_g[l], conv_ln_b[l])
        a, bm, cm = _s5_matrices(s5_lambda_re[l], s5_lambda_im[l], s5_b_re[l], s5_b_im[l],
                                 s5_c_re[l], s5_c_im[l], s5_log_dt[l])
        y_c = _s5(z, a, bm, cm, s5_d[l], s5_w_glu[l].astype(BF16))
        y_d = _attn(z, attn_q_norm_g[l], attn_k_norm_g[l], bias)
        x = _post(x, (y_a, y_b, y_c, y_d), mix_out_norm_g[l], w_out[l].astype(BF16), norm_mlp_g[l],
                  w_mlp_up[l].astype(BF16), w_mlp_down[l].astype(BF16))
    return x
```

```python
import functools
import math

import numpy as np
import jax
import jax.numpy as jnp
from jax import lax
from jax.experimental import pallas as pl
from jax.experimental.pallas import tpu as pltpu

F32 = jnp.float32
BF16 = jnp.bfloat16

D_MODEL = 1024
D_GROUP = 256
N_MIXERS = 4
HG_HEADS = 4
HG_DK = 64
LB_FLOOR = 1e-30
CONV_WIDTH = 31
S5_CH = 16
S5_GROUPS = 16
S5_STATE = 64
ATT_HEADS = 4
ATT_DH = 64
ATT_PATTERNS = ((128, 1), (512, 4), (2048, 16))
ATT_BLOCK = 128
REL_BUCKETS = 32
REL_MAX_DIST = 2048
NEG_BIG = -1e30
D_FF = 4 * D_MODEL
N_IN_SLICES = 10
D_IN = N_IN_SLICES * D_GROUP
EPS = 1e-6
LOG2E = 1.4426950408889634

LANES = 128
SUBLANES = 8
VMEM_LIMIT = 56 * 1024 * 1024

ROW_TILE = 512
HG_CHUNK = 128
HG_BLOCK = 512
CONV_SUB = 64
CONV_HALO = 32
S5_BLOCK = 256
S5_PITCH = S5_BLOCK + SUBLANES
S5_TILES = 2 * S5_GROUPS * S5_STATE // LANES
ATT_L = 2048


def _cparams(sem):
    return pltpu.CompilerParams(dimension_semantics=sem, vmem_limit_bytes=VMEM_LIMIT)


def _sigmoid(x):
    return 1.0 / (1.0 + jnp.exp(-x))


def _dot(a, b):
    return jnp.dot(a, b, preferred_element_type=F32)


def _dot_nt(a, b):
    return lax.dot_general(a, b, (((1,), (1,)), ((), ())), preferred_element_type=F32)


def _dot_tn(a, b):
    return lax.dot_general(a, b, (((0,), (0,)), ((), ())), preferred_element_type=F32)


def _inproj_kernel(x_ref, g_ref, w_ref, z_ref):
    x = x_ref[...]
    h = x * lax.rsqrt(jnp.mean(x * x, axis=-1, keepdims=True) + EPS) * g_ref[...]
    z_ref[...] = _dot(h.astype(BF16), w_ref[...]).astype(z_ref.dtype)


def _inproj(x, g, w):
    bsz, s, d = x.shape
    n = w.shape[1]
    return pl.pallas_call(
        _inproj_kernel,
        grid=(bsz, s // ROW_TILE),
        in_specs=[
            pl.BlockSpec((None, ROW_TILE, d), lambda b, i: (b, i, 0)),
            pl.BlockSpec((1, d), lambda b, i: (0, 0)),
            pl.BlockSpec((d, n), lambda b, i: (0, 0)),
        ],
        out_specs=pl.BlockSpec((None, ROW_TILE, n), lambda b, i: (b, i, 0)),
        out_shape=jax.ShapeDtypeStruct((bsz, s, n), BF16),
        compiler_params=_cparams(("parallel", "parallel")),
        name="inproj",
    )(x, g.reshape(1, d), w)


def _hold(gc, m):
    c, w = gc.shape
    if 2 * m >= SUBLANES:
        x = gc.reshape(c // (2 * m), 2 * m, w)
        return jnp.broadcast_to(x[:, m - 1:m, :], x.shape).reshape(c, w)
    x = gc.reshape(c // SUBLANES, SUBLANES, w)
    rows = lax.broadcasted_iota(jnp.int32, x.shape, 1)
    out = None
    for p in range(SUBLANES // (2 * m)):
        e = p * 2 * m + m - 1
        b = jnp.broadcast_to(x[:, e:e + 1, :], x.shape)
        out = b if out is None else jnp.where(rows >= p * 2 * m, b, out)
    return out.reshape(c, w)


def _hgrn_kernel(q_ref, f_ref, i_ref, g_ref, lb_ref, ng_ref, lvl_ref, o_ref, st_ref):
    @pl.when(pl.program_id(1) == 0)
    def _():
        st_ref[...] = jnp.zeros_like(st_ref)

    c = HG_CHUNK
    w = D_GROUP
    n_levels = int(math.log2(c))
    lane_head = lax.broadcasted_iota(jnp.int32, (c, w), 1) // HG_DK
    row = lax.broadcasted_iota(jnp.int32, (c, w), 0)
    blk = (lax.broadcasted_iota(jnp.int32, (w, w), 0) // HG_DK
           == lax.broadcasted_iota(jnp.int32, (w, w), 1) // HG_DK)
    ones_blk = jnp.where(blk, 1.0, 0.0).astype(BF16)
    lb = jnp.clip(lb_ref[...], 0.0, 1.0 - 1e-6)
    lb_floor = jnp.maximum(lb, LB_FLOOR)
    tri = jnp.where(lax.broadcasted_iota(jnp.int32, (c, c), 0) >= lax.broadcasted_iota(jnp.int32, (c, c), 1),
                    1.0, 0.0).astype(BF16)
    sgn = [jnp.where((row & (2 ** lv)) != 0, 1.0, -1.0) for lv in range(n_levels)]
    grp = SUBLANES

    def stack_heads(t):
        return jnp.concatenate(
            [jnp.where(lane_head == h, t, 0.0).astype(BF16) for h in range(HG_HEADS)], axis=0)

    def chunk(r0, st):
        q = q_ref[r0:r0 + c, :].astype(F32)
        fz = f_ref[r0:r0 + c, :].astype(F32)
        v = i_ref[r0:r0 + c, :].astype(F32)
        gate = g_ref[r0:r0 + c, :].astype(F32)
        gated = (1.0 - lb) * _sigmoid(fz)
        lf = jnp.log2(lb_floor + gated)
        kk = (1.0 - lb) - gated
        hi = lf.astype(BF16)
        r1 = lf - hi.astype(F32)
        mid = r1.astype(BF16)
        lo = (r1 - mid.astype(F32)).astype(BF16)
        gc = _dot(tri, hi) + _dot(tri, mid) + _dot(tri, lo)

        q_b = q.astype(BF16)
        k_b = [jnp.where(lane_head == h, kk, 0.0).astype(BF16) for h in range(HG_HEADS)]
        n_grp = c // grp
        scores = [None] * n_grp
        for lv in range(n_levels):
            m = 2 ** lv
            e = jnp.exp2((gc - _hold(gc, m)) * sgn[lv]).astype(BF16)
            p_lv = _dot_nt(q_b * e, jnp.concatenate([kb * e for kb in k_b], axis=0))
            for gi in range(n_grp):
                if m >= grp and (gi * grp) & m == 0:
                    continue
                rs = slice(gi * grp, (gi + 1) * grp)
                old = 0.0 if scores[gi] is None else scores[gi]
                scores[gi] = jnp.where(lvl_ref[rs, :] == lv, p_lv[rs], old)
        o = _dot(jnp.concatenate(scores, axis=0).astype(BF16), stack_heads(v))
        o = o + _dot((q * kk).astype(BF16), ones_blk) * v
        o = o + _dot_nt((q * jnp.exp2(gc)).astype(BF16), st.astype(BF16))
        g_last = gc[c - 1:c, :]
        upd = _dot_tn(v.astype(BF16), (kk * jnp.exp2(g_last - gc)).astype(BF16))
        st = st * jnp.exp2(g_last) + jnp.where(blk, upd, 0.0)

        ms = _dot((o * o).astype(BF16), ones_blk) * (1.0 / HG_DK)
        y = o * lax.rsqrt(ms + EPS) * ng_ref[...]
        o_ref[r0:r0 + c, :] = y * (gate * _sigmoid(gate))
        return st

    st = st_ref[...]
    for r0 in range(0, q_ref.shape[0], c):
        st = chunk(r0, st)
    st_ref[...] = st


def _hgrn_levels():
    t = np.arange(HG_CHUNK)[:, None]
    s = np.arange(HG_CHUNK)[None, :]
    x = np.maximum(t ^ s, 1)
    lv = np.where(s < t, np.floor(np.log2(x)).astype(np.int32), -1).astype(np.int32)
    return np.tile(lv, (1, HG_HEADS))


def _hgrn(z, lb, norm_g):
    bsz, s, _ = z.shape
    tb = min(HG_BLOCK, s)

    def zspec(j):
        return pl.BlockSpec((None, tb, D_GROUP), lambda b, i, j=j: (b, i, j))

    const = lambda shape: pl.BlockSpec(shape, lambda b, i: (0, 0))
    return pl.pallas_call(
        _hgrn_kernel,
        grid=(bsz, s // tb),
        in_specs=[zspec(0), zspec(1), zspec(2), zspec(3),
                  const((1, D_GROUP)), const((1, D_GROUP)), const((HG_CHUNK, HG_HEADS * HG_CHUNK))],
        out_specs=pl.BlockSpec((None, tb, D_GROUP), lambda b, i: (b, i, 0)),
        out_shape=jax.ShapeDtypeStruct((bsz, s, D_GROUP), F32),
        scratch_shapes=[pltpu.VMEM((D_GROUP, D_GROUP), F32)],
        compiler_params=_cparams(("parallel", "arbitrary")),
        name="hgrn2",
    )(z, z, z, z, lb.reshape(1, D_GROUP), norm_g.reshape(1, D_GROUP), jnp.asarray(_hgrn_levels()))


def _conv_tile_steps(val_ref, gate_ref, w_ref, b_ref, lg_ref, lbias_ref, o_ref, h_ref, first):
    tb = val_ref.shape[0]
    base = CONV_HALO - (CONV_WIDTH - 1)

    def glu():
        if first:
            h_ref[0:CONV_HALO, :] = jnp.zeros((CONV_HALO, D_GROUP), F32)
        else:
            h_ref[0:CONV_HALO, :] = h_ref[tb:tb + CONV_HALO, :]
        h_ref[CONV_HALO:CONV_HALO + tb, :] = val_ref[...].astype(F32) * _sigmoid(gate_ref[...].astype(F32))

    def sub_tile(t0):
        acc = jnp.broadcast_to(b_ref[...], (CONV_SUB, D_GROUP))
        for c in range(SUBLANES):
            part = None
            rows = CONV_SUB + (SUBLANES if c else 0)
            for j in range(CONV_WIDTH):
                if (base + j) % SUBLANES != c:
                    continue
                r0 = t0 + base + j - c
                term = w_ref[j:j + 1, :] * h_ref[r0:r0 + rows, :]
                part = term if part is None else part + term
            acc = acc + part[c:c + CONV_SUB, :]
        mu = jnp.mean(acc, axis=-1, keepdims=True)
        d = acc - mu
        var = jnp.mean(d * d, axis=-1, keepdims=True)
        y = d * lax.rsqrt(var + EPS) * lg_ref[...] + lbias_ref[...]
        o_ref[t0:t0 + CONV_SUB, :] = y * _sigmoid(y)

    return [glu] + [functools.partial(sub_tile, t0) for t0 in range(0, tb, CONV_SUB)]


def _s5_kernel(u_ref, a_ref, bm_ref, cm_ref, d_ref, wg_ref, o_ref, bu_ref, hs_ref, st_ref):
    nb, t_len, _ = u_ref.shape
    half = S5_TILES // 2

    @pl.when(pl.program_id(0) == 0)
    def _():
        st_ref[...] = jnp.zeros_like(st_ref)

    ar = a_ref[0:half, :]
    ai = a_ref[half:S5_TILES, :]

    def input_map(bs):
        for b in bs:
            bu = _dot(u_ref[b], bm_ref[...])
            for i in range(S5_TILES):
                bu_ref[b, i * S5_PITCH:i * S5_PITCH + t_len, :] = bu[:, i * LANES:(i + 1) * LANES]

    def scan(bs):
        hs = [(st_ref[b, 0:half, :], st_ref[b, half:S5_TILES, :]) for b in bs]
        for t in range(t_len):
            for k, b in enumerate(bs):
                hr, hi = hs[k]
                br = bu_ref[b, pl.ds(t, half, stride=S5_PITCH), :]
                bi = bu_ref[b, pl.ds(half * S5_PITCH + t, half, stride=S5_PITCH), :]
                nr = ar * hr - ai * hi + br
                ni = ar * hi + ai * hr + bi
                hs_ref[b, pl.ds(t, half, stride=S5_PITCH), :] = nr
                hs_ref[b, pl.ds(half * S5_PITCH + t, half, stride=S5_PITCH), :] = ni
                hs[k] = (nr, ni)
        for k, b in enumerate(bs):
            st_ref[b, 0:half, :] = hs[k][0]
            st_ref[b, half:S5_TILES, :] = hs[k][1]

    def output_map(bs):
        for b in bs:
            u = u_ref[b].astype(F32)
            hcat = jnp.concatenate(
                [hs_ref[b, i * S5_PITCH:i * S5_PITCH + t_len, :] for i in range(S5_TILES)], axis=1)
            y = _dot(hcat.astype(BF16), cm_ref[...]) + d_ref[...] * u
            y = 0.5 * y * (1.0 + jnp.tanh(math.sqrt(2.0 / math.pi) * (y + 0.044715 * (y * y * y))))
            o_ref[b] = y * _sigmoid(_dot(y.astype(BF16), wg_ref[...]))

    g0 = tuple(range(nb))
    input_map(g0)
    scan(g0)
    output_map(g0)


def _s5_matrices(lam_re, lam_im, b_re, b_im, c_re, c_im, log_dt):
    dt = jnp.exp(log_dt.astype(F32))[:, None]
    lr = lam_re.astype(F32)
    li = lam_im.astype(F32)
    mag = jnp.exp(lr * dt)
    a_re = mag * jnp.cos(li * dt)
    a_im = mag * jnp.sin(li * dt)
    den = lr * lr + li * li
    coef_re = ((a_re - 1.0) * lr + a_im * li) / den
    coef_im = (a_im * lr - (a_re - 1.0) * li) / den
    br = b_re.astype(F32)
    bi = b_im.astype(F32)
    bbar_re = coef_re[..., None] * br - coef_im[..., None] * bi
    bbar_im = coef_re[..., None] * bi + coef_im[..., None] * br
    eye = jnp.eye(S5_GROUPS, dtype=F32)
    n_state = S5_GROUPS * S5_STATE

    def b_block(bb):
        return jnp.einsum('gpc,gh->gchp', bb, eye).reshape(D_GROUP, n_state)

    def c_block(cc):
        return jnp.einsum('gcp,gh->gphc', cc, eye).reshape(n_state, D_GROUP)

    bm = jnp.concatenate([b_block(bbar_re), b_block(bbar_im)], axis=1).astype(BF16)
    cm = jnp.concatenate([c_block(c_re.astype(F32)), -c_block(c_im.astype(F32))], axis=0).astype(BF16)
    a = jnp.concatenate([a_re.reshape(n_state // LANES, LANES), a_im.reshape(n_state // LANES, LANES)], 0)
    return a, bm, cm


def _s5(z, a, bm, cm, d_skip, w_glu):
    bsz, s, _ = z.shape
    tb = min(S5_BLOCK, s)
    assert tb == S5_BLOCK
    const = lambda shape: pl.BlockSpec(shape, lambda i: (0, 0))
    n_state2 = 2 * S5_GROUPS * S5_STATE
    return pl.pallas_call(
        _s5_kernel,
        grid=(s // tb,),
        in_specs=[pl.BlockSpec((bsz, tb, D_GROUP), lambda i: (0, i, 6)),
                  const((S5_TILES, LANES)), const((D_GROUP, n_state2)), const((n_state2, D_GROUP)),
                  const((1, D_GROUP)), const((D_GROUP, D_GROUP))],
        out_specs=pl.BlockSpec((bsz, tb, D_GROUP), lambda i: (0, i, 0)),
        out_shape=jax.ShapeDtypeStruct((bsz, s, D_GROUP), F32),
        scratch_shapes=[pltpu.VMEM((bsz, S5_TILES * S5_PITCH, LANES), F32),
                        pltpu.VMEM((bsz, S5_TILES * S5_PITCH, LANES), F32),
                        pltpu.VMEM((bsz, S5_TILES, LANES), F32)],
        compiler_params=_cparams(("arbitrary",)),
        name="s5",
    )(z, a, bm, cm, d_skip.reshape(1, D_GROUP), w_glu)


def _t5_bucket_np(dist):
    max_exact = REL_BUCKETS // 2
    distf = np.maximum(dist, max_exact).astype(np.float32)
    large = max_exact + (np.log(distf / np.float32(max_exact)) / np.float32(math.log(REL_MAX_DIST / max_exact))
                         * np.float32(REL_BUCKETS - max_exact)).astype(np.int32)
    large = np.minimum(large, REL_BUCKETS - 1)
    return np.where(dist < max_exact, dist, large)


def _attn_bias(rel_bias):
    blk = ATT_BLOCK
    period = 3 * blk + 1
    k = np.arange(-(blk - 1), 2 * blk)
    tabs = []
    for window, dilation in ATT_PATTERNS:
        delta = blk - k
        valid = (delta >= 0) & (delta <= window // dilation)
        bucket = _t5_bucket_np(np.maximum(delta, 0) * dilation)
        vals = jnp.where(valid[:, None], rel_bias.astype(F32)[bucket] * LOG2E, NEG_BIG)
        v = jnp.full((period, ATT_HEADS), NEG_BIG, F32).at[k % period].set(vals).T
        tiled = jnp.tile(v, (1, blk))[:, :blk * (period - 1)]
        tabs.append(tiled.reshape(ATT_HEADS, blk, period - 1)[:, :, :2 * blk])
    return jnp.concatenate(tabs, axis=0)


def _attn_kernel(q_ref, k_ref, v_ref, qg_ref, kg_ref, bias_ref, o_ref,
                 qn_ref, kn_ref, vn_ref, on_ref, lse_ref, btab_ref, sc0_ref, sc1_ref):
    L = q_ref.shape[0]
    blk = ATT_BLOCK
    n_pat = len(ATT_PATTERNS)
    step = pl.program_id(1)
    first = step == 0
    half_lane = lax.broadcasted_iota(jnp.int32, (blk, LANES), 1) < ATT_DH
    ones_blk = jnp.where(
        lax.broadcasted_iota(jnp.int32, (LANES, LANES), 0) // ATT_DH
        == lax.broadcasted_iota(jnp.int32, (LANES, LANES), 1) // ATT_DH, 1.0, 0.0).astype(BF16)

    @pl.when(first)
    def _():
        kn_ref[:, 0:L, :] = jnp.zeros((2, L, LANES), F32)
        vn_ref[:, 0:L, :] = jnp.zeros((2, L, LANES), F32)
        prev_cols = lax.broadcasted_iota(jnp.int32, (blk, 2 * blk), 1) < blk
        for t in range(n_pat * ATT_HEADS):
            btab_ref[0, t] = bias_ref[t]
            btab_ref[1, t] = jnp.where(prev_cols, NEG_BIG, bias_ref[t])

    @pl.when(step == 1)
    def _():
        for t in range(n_pat * ATT_HEADS):
            btab_ref[1, t] = bias_ref[t]

    @pl.when(jnp.logical_not(first))
    def _():
        kn_ref[:, 0:L, :] = kn_ref[:, L:2 * L, :]
        vn_ref[:, 0:L, :] = vn_ref[:, L:2 * L, :]

    def head_norm(x, g):
        ms = _dot((x * x).astype(BF16), ones_blk) * (1.0 / ATT_DH)
        return x * lax.rsqrt(ms + EPS) * g

    for p in range(2):
        ls = slice(p * LANES, (p + 1) * LANES)
        qn_ref[p] = head_norm(q_ref[:, ls].astype(F32), qg_ref[...]) * (ATT_DH ** -0.5 * LOG2E)
        kn_ref[p, L:2 * L, :] = head_norm(k_ref[:, ls].astype(F32), kg_ref[...])
        vn_ref[p, L:2 * L, :] = v_ref[:, ls].astype(F32)

    def rows(start, n, dil):
        return pl.ds(start, n) if dil == 1 else pl.ds(start, n, stride=dil)

    ones_cols = jnp.ones((2 * blk, LANES), BF16)
    n_combo = L // blk

    def offsets(ci, dil):
        q0 = (ci % dil) + dil * blk * (ci // dil)
        return q0, L + q0 - dil * blk

    def scores(pi, ci, s_ref):
        dil = ATT_PATTERNS[pi][1]
        q0, k0 = offsets(ci, dil)
        tsel = jnp.where(ci < dil, 1, 0)
        for p in range(2):
            qb = qn_ref[p, rows(q0, blk, dil), :]
            kb = kn_ref[p, rows(k0, 2 * blk, dil), :].astype(BF16)
            for hh in range(2):
                sel = half_lane if hh == 0 else jnp.logical_not(half_lane)
                h = 2 * p + hh
                s_ref[h] = (_dot_nt(jnp.where(sel, qb, 0.0).astype(BF16), kb)
                            + btab_ref[tsel, pi * ATT_HEADS + h])

    def softmax_pv(pi, ci, s_ref):
        dil = ATT_PATTERNS[pi][1]
        q0, k0 = offsets(ci, dil)
        for p in range(2):
            vb = jnp.concatenate([vn_ref[p, rows(k0, 2 * blk, dil), :].astype(BF16), ones_cols], axis=1)
            mxs, ovs = [], []
            for hh in range(2):
                sc = s_ref[2 * p + hh]
                mx = jnp.max(sc, axis=-1, keepdims=True)
                ovs.append(_dot(jnp.exp2(sc - mx).astype(BF16), vb))
                mxs.append(mx)
            den = jnp.where(half_lane, ovs[0][:, LANES:], ovs[1][:, LANES:])
            num = jnp.where(half_lane, ovs[0][:, :LANES], ovs[1][:, :LANES])
            on_ref[pi, p, rows(q0, blk, dil), :] = num * (1.0 / den)
            lse_ref[pi, p, rows(q0, blk, dil), :] = jnp.where(half_lane, mxs[0], mxs[1]) + jnp.log2(den)

    scores(0, 0, sc0_ref)
    for pi in range(n_pat):
        def pair(j, carry, pi=pi):
            scores(pi, 2 * j + 1, sc1_ref)
            softmax_pv(pi, 2 * j, sc0_ref)
            scores(pi, 2 * j + 2, sc0_ref)
            softmax_pv(pi, 2 * j + 1, sc1_ref)
            return carry

        lax.fori_loop(0, n_combo // 2 - 1, pair, 0)
        scores(pi, n_combo - 1, sc1_ref)
        softmax_pv(pi, n_combo - 2, sc0_ref)
        if pi + 1 < n_pat:
            scores(pi + 1, 0, sc0_ref)
        softmax_pv(pi, n_combo - 1, sc1_ref)

    def merge(ti, carry):
        t0 = pl.multiple_of(ti * blk, blk)
        for p in range(2):
            ls_ = [lse_ref[pi, p, pl.ds(t0, blk), :] for pi in range(n_pat)]
            mx = functools.reduce(jnp.maximum, ls_)
            ws = [jnp.exp2(x - mx) for x in ls_]
            num = sum(ws[pi] * on_ref[pi, p, pl.ds(t0, blk), :] for pi in range(n_pat))
            o_ref[pl.ds(t0, blk), p * LANES:(p + 1) * LANES] = num / sum(ws)
        return carry

    lax.fori_loop(0, L // blk, merge, 0)


def _attn(z, q_g, k_g, bias):
    bsz, s, _ = z.shape
    L = ATT_L
    assert s % L == 0
    gq = jnp.concatenate([q_g, q_g]).reshape(1, LANES).astype(F32)
    gk = jnp.concatenate([k_g, k_g]).reshape(1, LANES).astype(F32)

    def zspec(j):
        return pl.BlockSpec((None, L, D_GROUP), lambda b, i, j=j: (b, i, j))

    return pl.pallas_call(
        _attn_kernel,
        grid=(bsz, s // L),
        in_specs=[zspec(7), zspec(8), zspec(9),
                  pl.BlockSpec((1, LANES), lambda b, i: (0, 0)),
                  pl.BlockSpec((1, LANES), lambda b, i: (0, 0)),
                  pl.BlockSpec(bias.shape, lambda b, i: (0, 0, 0))],
        out_specs=pl.BlockSpec((None, L, D_GROUP), lambda b, i: (b, i, 0)),
        out_shape=jax.ShapeDtypeStruct((bsz, s, D_GROUP), F32),
        scratch_shapes=[pltpu.VMEM((2, L, LANES), F32),
                        pltpu.VMEM((2, 2 * L, LANES), F32),
                        pltpu.VMEM((2, 2 * L, LANES), F32),
                        pltpu.VMEM((len(ATT_PATTERNS), 2, L, LANES), F32),
                        pltpu.VMEM((len(ATT_PATTERNS), 2, L, LANES), F32),
                        pltpu.VMEM((2,) + bias.shape, F32),
                        pltpu.VMEM((ATT_HEADS, ATT_BLOCK, 2 * ATT_BLOCK), F32),
                        pltpu.VMEM((ATT_HEADS, ATT_BLOCK, 2 * ATT_BLOCK), F32)],
        compiler_params=_cparams(("parallel", "arbitrary")),
        name="dilated_attn",
    )(z, z, z, gq, gk, bias)


def _post_kernel(x_ref, ya_ref, yc_ref, yd_ref, val0_ref, gate0_ref, valn_ref, gaten_ref,
                 cw_ref, cb_ref, clg_ref, clb_ref, gmix_ref, wout_ref, gmlp_ref, wup_ref, wdn_ref,
                 o_ref, h_ref, yb_ref):
    conv_params = (cw_ref, cb_ref, clg_ref, clb_ref)

    @pl.when(pl.program_id(1) == 0)
    def _():
        for step in _conv_tile_steps(val0_ref, gate0_ref, *conv_params, yb_ref, h_ref, first=True):
            step()

    conv_steps = _conv_tile_steps(valn_ref, gaten_ref, *conv_params, yb_ref, h_ref, first=False)
    ys = (ya_ref[...], yb_ref[...], yc_ref[...], yd_ref[...])
    conv_steps.pop(0)()
    acc = x_ref[...]
    for j, y in enumerate(ys):
        cs = slice(j * D_GROUP, (j + 1) * D_GROUP)
        yn = y * lax.rsqrt(jnp.mean(y * y, axis=-1, keepdims=True) + EPS) * gmix_ref[:, cs]
        acc = acc + _dot(yn.astype(BF16), wout_ref[cs, :])
    h = (acc * lax.rsqrt(jnp.mean(acc * acc, axis=-1, keepdims=True) + EPS) * gmlp_ref[...]).astype(BF16)
    mlp = jnp.zeros_like(acc)
    ff_tile = D_MODEL
    per_dot = len(conv_steps) * ff_tile // (2 * D_FF)
    for f0 in range(0, D_FF, ff_tile):
        hm = jnp.maximum(_dot(h, wup_ref[:, f0:f0 + ff_tile]), 0.0)
        for _ in range(per_dot):
            conv_steps.pop(0)()
        mlp = mlp + _dot((hm * hm).astype(BF16), wdn_ref[f0:f0 + ff_tile, :])
        for _ in range(per_dot):
            conv_steps.pop(0)()
    assert not conv_steps
    o_ref[...] = acc + mlp


def _post(x, z, y_a, y_c, y_d, conv_w, conv_b, ln_g, ln_b, gmix, wout, gmlp, wup, wdn):
    bsz, s, d = x.shape
    n_tiles = s // ROW_TILE
    row = lambda n: pl.BlockSpec((None, ROW_TILE, n), lambda b, i: (b, i, 0))
    const = lambda shape: pl.BlockSpec(shape, lambda b, i: (0, 0), pipeline_mode=pl.Buffered(1))
    z_first = lambda j: pl.BlockSpec((None, ROW_TILE, D_GROUP), lambda b, i: (b, 0, j))
    z_next = lambda j: pl.BlockSpec((None, ROW_TILE, D_GROUP),
                                    lambda b, i: (b, jnp.minimum(i + 1, n_tiles - 1), j))
    vec = lambda v: v.reshape(1, -1)
    return pl.pallas_call(
        _post_kernel,
        grid=(bsz, n_tiles),
        in_specs=[row(d), row(D_GROUP), row(D_GROUP), row(D_GROUP),
                  z_first(4), z_first(5), z_next(4), z_next(5),
                  const((CONV_WIDTH, D_GROUP)), const((1, D_GROUP)), const((1, D_GROUP)), const((1, D_GROUP)),
                  const((1, d)), const((d, d)), const((1, d)), const((d, D_FF)), const((D_FF, d))],
        out_specs=row(d),
        out_shape=jax.ShapeDtypeStruct((bsz, s, d), F32),
        scratch_shapes=[pltpu.VMEM((CONV_HALO + ROW_TILE, D_GROUP), F32),
                        pltpu.VMEM((ROW_TILE, D_GROUP), F32)],
        compiler_params=_cparams(("parallel", "arbitrary")),
        name="outproj_mlp",
    )(x, y_a, y_c, y_d, z, z, z, z, conv_w, vec(conv_b), vec(ln_g), vec(ln_b),
      vec(gmix), wout, vec(gmlp), wup, wdn)


def kernel(x, norm_mix_g, w_in, hgrn_lb_logits, hgrn_norm_g, conv_w, conv_b, conv_ln_g, conv_ln_b, s5_lambda_re, s5_lambda_im, s5_b_re, s5_b_im, s5_c_re, s5_c_im, s5_d, s5_log_dt, s5_w_glu, attn_q_norm_g, attn_k_norm_g, rel_bias, mix_out_norm_g, w_out, norm_mlp_g, w_mlp_up, w_mlp_down):
    depth = w_in.shape[0]
    lb_sm = jax.nn.softmax(hgrn_lb_logits.astype(F32), axis=0)
    lb_all = jnp.maximum(jnp.cumsum(lb_sm, axis=0) - lb_sm[0], 0.0)
    bias = _attn_bias(rel_bias)
    for l in range(depth):
        z = _inproj(x, norm_mix_g[l], w_in[l].astype(BF16))
        y_a = _hgrn(z, lb_all[l], hgrn_norm_g[l])
        a, bm, cm = _s5_matrices(s5_lambda_re[l], s5_lambda_im[l], s5_b_re[l], s5_b_im[l],
                                 s5_c_re[l], s5_c_im[l], s5_log_dt[l])
        y_c = _s5(z, a, bm, cm, s5_d[l], s5_w_glu[l].astype(BF16))
        y_d = _attn(z, attn_q_norm_g[l], attn_k_norm_g[l], bias)
        x = _post(x, z, y_a, y_c, y_d, conv_w[l], conv_b[l], conv_ln_g[l], conv_ln_b[l],
                  mix_out_norm_g[l], w_out[l].astype(BF16), norm_mlp_g[l],
                  w_mlp_up[l].astype(BF16), w_mlp_down[l].astype(BF16))
    return x
```

```python
import functools
import math

import numpy as np
import jax
import jax.numpy as jnp
from jax import lax
from jax.experimental import pallas as pl
from jax.experimental.pallas import tpu as pltpu

F32 = jnp.float32
BF16 = jnp.bfloat16

D_MODEL = 1024
D_GROUP = 256
N_MIXERS = 4
HG_HEADS = 4
HG_DK = 64
LB_FLOOR = 1e-30
CONV_WIDTH = 31
S5_CH = 16
S5_GROUPS = 16
S5_STATE = 64
ATT_HEADS = 4
ATT_DH = 64
ATT_PATTERNS = ((128, 1), (512, 4), (2048, 16))
ATT_BLOCK = 128
REL_BUCKETS = 32
REL_MAX_DIST = 2048
NEG_BIG = -1e30
D_FF = 4 * D_MODEL
N_IN_SLICES = 10
D_IN = N_IN_SLICES * D_GROUP
EPS = 1e-6
LOG2E = 1.4426950408889634

LANES = 128
SUBLANES = 8
VMEM_LIMIT = 56 * 1024 * 1024

ROW_TILE = 512
HG_CHUNK = 128
HG_BLOCK = 512
CONV_SUB = 64
CONV_HALO = 32
S5_BLOCK = 256
S5_PITCH = S5_BLOCK + SUBLANES
S5_TILES = 2 * S5_GROUPS * S5_STATE // LANES
ATT_L = 2048
ATT_TAIL = 512
ATT_P16 = ATT_BLOCK + SUBLANES


def _cparams(sem):
    return pltpu.CompilerParams(dimension_semantics=sem, vmem_limit_bytes=VMEM_LIMIT)


def _sigmoid(x):
    return 1.0 / (1.0 + jnp.exp(-x))


def _dot(a, b):
    return jnp.dot(a, b, preferred_element_type=F32)


def _dot_nt(a, b):
    return lax.dot_general(a, b, (((1,), (1,)), ((), ())), preferred_element_type=F32)


def _dot_tn(a, b):
    return lax.dot_general(a, b, (((0,), (0,)), ((), ())), preferred_element_type=F32)


def _inproj_kernel(x_ref, g_ref, w_ref, z_ref):
    x = x_ref[...]
    h = x * lax.rsqrt(jnp.mean(x * x, axis=-1, keepdims=True) + EPS) * g_ref[...]
    z_ref[...] = _dot(h.astype(BF16), w_ref[...]).astype(z_ref.dtype)


def _inproj(x, g, w_all, layer):
    bsz, s, d = x.shape
    n = w_all.shape[2]
    return pl.pallas_call(
        _inproj_kernel,
        grid=(bsz, s // ROW_TILE),
        in_specs=[
            pl.BlockSpec((None, ROW_TILE, d), lambda b, i: (b, i, 0)),
            pl.BlockSpec((1, d), lambda b, i: (0, 0)),
            pl.BlockSpec((None, d, n), lambda b, i: (layer, 0, 0)),
        ],
        out_specs=pl.BlockSpec((None, ROW_TILE, n), lambda b, i: (b, i, 0)),
        out_shape=jax.ShapeDtypeStruct((bsz, s, n), BF16),
        compiler_params=_cparams(("parallel", "parallel")),
        name="inproj",
    )(x, g.reshape(1, d), w_all)


def _hold(gc, m):
    c, w = gc.shape
    if 2 * m >= SUBLANES:
        x = gc.reshape(c // (2 * m), 2 * m, w)
        return jnp.broadcast_to(x[:, m - 1:m, :], x.shape).reshape(c, w)
    x = gc.reshape(c // SUBLANES, SUBLANES, w)
    rows = lax.broadcasted_iota(jnp.int32, x.shape, 1)
    out = None
    for p in range(SUBLANES // (2 * m)):
        e = p * 2 * m + m - 1
        b = jnp.broadcast_to(x[:, e:e + 1, :], x.shape)
        out = b if out is None else jnp.where(rows >= p * 2 * m, b, out)
    return out.reshape(c, w)


def _hgrn_kernel(q_ref, f_ref, i_ref, g_ref, lb_ref, ng_ref, lvl_ref, o_ref, st_ref):
    @pl.when(pl.program_id(1) == 0)
    def _():
        st_ref[...] = jnp.zeros_like(st_ref)

    c = HG_CHUNK
    w = D_GROUP
    n_levels = int(math.log2(c))
    lane_head = lax.broadcasted_iota(jnp.int32, (c, w), 1) // HG_DK
    row = lax.broadcasted_iota(jnp.int32, (c, w), 0)
    blk = (lax.broadcasted_iota(jnp.int32, (w, w), 0) // HG_DK
           == lax.broadcasted_iota(jnp.int32, (w, w), 1) // HG_DK)
    ones_blk = jnp.where(blk, 1.0, 0.0).astype(BF16)
    lb = jnp.clip(lb_ref[...], 0.0, 1.0 - 1e-6)
    lb_floor = jnp.maximum(lb, LB_FLOOR)
    tri = jnp.where(lax.broadcasted_iota(jnp.int32, (c, c), 0) >= lax.broadcasted_iota(jnp.int32, (c, c), 1),
                    1.0, 0.0).astype(BF16)
    sgn = [jnp.where((row & (2 ** lv)) != 0, 1.0, -1.0) for lv in range(n_levels)]
    grp = SUBLANES

    def stack_heads(t):
        return jnp.concatenate(
            [jnp.where(lane_head == h, t, 0.0).astype(BF16) for h in range(HG_HEADS)], axis=0)

    def chunk(r0, st):
        q = q_ref[r0:r0 + c, :].astype(F32)
        fz = f_ref[r0:r0 + c, :].astype(F32)
        v = i_ref[r0:r0 + c, :].astype(F32)
        gate = g_ref[r0:r0 + c, :].astype(F32)
        gated = (1.0 - lb) * _sigmoid(fz)
        lf = jnp.log2(lb_floor + gated)
        kk = (1.0 - lb) - gated
        hi = lf.astype(BF16)
        r1 = lf - hi.astype(F32)
        mid = r1.astype(BF16)
        lo = (r1 - mid.astype(F32)).astype(BF16)
        gc = _dot(tri, hi) + _dot(tri, mid) + _dot(tri, lo)

        q_b = q.astype(BF16)
        k_b = [jnp.where(lane_head == h, kk, 0.0).astype(BF16) for h in range(HG_HEADS)]
        n_grp = c // grp
        scores = [None] * n_grp
        for lv in range(n_levels):
            m = 2 ** lv
            e = jnp.exp2((gc - _hold(gc, m)) * sgn[lv]).astype(BF16)
            p_lv = _dot_nt(q_b * e, jnp.concatenate([kb * e for kb in k_b], axis=0))
            for gi in range(n_grp):
                if m >= grp and (gi * grp) & m == 0:
                    continue
                rs = slice(gi * grp, (gi + 1) * grp)
                old = 0.0 if scores[gi] is None else scores[gi]
                scores[gi] = jnp.where(lvl_ref[rs, :] == lv, p_lv[rs], old)
        o = _dot(jnp.concatenate(scores, axis=0).astype(BF16), stack_heads(v))
        o = o + _dot((q * kk).astype(BF16), ones_blk) * v
        o = o + _dot_nt((q * jnp.exp2(gc)).astype(BF16), st.astype(BF16))
        g_last = gc[c - 1:c, :]
        upd = _dot_tn(v.astype(BF16), (kk * jnp.exp2(g_last - gc)).astype(BF16))
        st = st * jnp.exp2(g_last) + jnp.where(blk, upd, 0.0)

        ms = _dot((o * o).astype(BF16), ones_blk) * (1.0 / HG_DK)
        y = o * lax.rsqrt(ms + EPS) * ng_ref[...]
        o_ref[r0:r0 + c, :] = y * (gate * _sigmoid(gate))
        return st

    st = st_ref[...]
    for r0 in range(0, q_ref.shape[0], c):
        st = chunk(r0, st)
    st_ref[...] = st


def _hgrn_levels():
    t = np.arange(HG_CHUNK)[:, None]
    s = np.arange(HG_CHUNK)[None, :]
    x = np.maximum(t ^ s, 1)
    lv = np.where(s < t, np.floor(np.log2(x)).astype(np.int32), -1).astype(np.int32)
    return np.tile(lv, (1, HG_HEADS))


def _hgrn(z, lb, norm_g):
    bsz, s, _ = z.shape
    tb = min(HG_BLOCK, s)

    def zspec(j):
        return pl.BlockSpec((None, tb, D_GROUP), lambda b, i, j=j: (b, i, j))

    const = lambda shape: pl.BlockSpec(shape, lambda b, i: (0, 0))
    return pl.pallas_call(
        _hgrn_kernel,
        grid=(bsz, s // tb),
        in_specs=[zspec(0), zspec(1), zspec(2), zspec(3),
                  const((1, D_GROUP)), const((1, D_GROUP)), const((HG_CHUNK, HG_HEADS * HG_CHUNK))],
        out_specs=pl.BlockSpec((None, tb, D_GROUP), lambda b, i: (b, i, 0)),
        out_shape=jax.ShapeDtypeStruct((bsz, s, D_GROUP), F32),
        scratch_shapes=[pltpu.VMEM((D_GROUP, D_GROUP), F32)],
        compiler_params=_cparams(("parallel", "arbitrary")),
        name="hgrn2",
    )(z, z, z, z, lb.reshape(1, D_GROUP), norm_g.reshape(1, D_GROUP), jnp.asarray(_hgrn_levels()))


def _conv_tile_steps(val_ref, gate_ref, w_ref, b_ref, lg_ref, lbias_ref, o_ref, h_ref, first):
    tb = val_ref.shape[0]
    base = CONV_HALO - (CONV_WIDTH - 1)

    def glu():
        if first:
            h_ref[0:CONV_HALO, :] = jnp.zeros((CONV_HALO, D_GROUP), F32)
        else:
            h_ref[0:CONV_HALO, :] = h_ref[tb:tb + CONV_HALO, :]
        h_ref[CONV_HALO:CONV_HALO + tb, :] = val_ref[...].astype(F32) * _sigmoid(gate_ref[...].astype(F32))

    def sub_tile(t0):
        acc = jnp.broadcast_to(b_ref[...], (CONV_SUB, D_GROUP))
        for c in range(SUBLANES):
            part = None
            rows = CONV_SUB + (SUBLANES if c else 0)
            for j in range(CONV_WIDTH):
                if (base + j) % SUBLANES != c:
                    continue
                r0 = t0 + base + j - c
                term = w_ref[j:j + 1, :] * h_ref[r0:r0 + rows, :]
                part = term if part is None else part + term
            acc = acc + part[c:c + CONV_SUB, :]
        mu = jnp.mean(acc, axis=-1, keepdims=True)
        d = acc - mu
        var = jnp.mean(d * d, axis=-1, keepdims=True)
        y = d * lax.rsqrt(var + EPS) * lg_ref[...] + lbias_ref[...]
        o_ref[t0:t0 + CONV_SUB, :] = y * _sigmoid(y)

    return [glu] + [functools.partial(sub_tile, t0) for t0 in range(0, tb, CONV_SUB)]


def _s5_kernel(u_ref, a_ref, bm_ref, cm_ref, d_ref, wg_ref, o_ref, bu_ref, hs_ref, st_ref):
    nb, t_len, _ = u_ref.shape
    half = S5_TILES // 2

    @pl.when(pl.program_id(0) == 0)
    def _():
        st_ref[...] = jnp.zeros_like(st_ref)

    ar = a_ref[0:half, :]
    ai = a_ref[half:S5_TILES, :]

    for b in range(nb):
        bu = _dot(u_ref[b], bm_ref[...])
        for i in range(S5_TILES):
            bu_ref[b, i * S5_PITCH:i * S5_PITCH + t_len, :] = bu[:, i * LANES:(i + 1) * LANES]

    hs = [(st_ref[b, 0:half, :], st_ref[b, half:S5_TILES, :]) for b in range(nb)]
    for t in range(t_len):
        for b in range(nb):
            hr, hi = hs[b]
            br = bu_ref[b, pl.ds(t, half, stride=S5_PITCH), :]
            bi = bu_ref[b, pl.ds(half * S5_PITCH + t, half, stride=S5_PITCH), :]
            nr = ar * hr - ai * hi + br
            ni = ar * hi + ai * hr + bi
            hs_ref[b, pl.ds(t, half, stride=S5_PITCH), :] = nr
            hs_ref[b, pl.ds(half * S5_PITCH + t, half, stride=S5_PITCH), :] = ni
            hs[b] = (nr, ni)
    for b in range(nb):
        st_ref[b, 0:half, :] = hs[b][0]
        st_ref[b, half:S5_TILES, :] = hs[b][1]

    for b in range(nb):
        u = u_ref[b].astype(F32)
        hcat = jnp.concatenate(
            [hs_ref[b, i * S5_PITCH:i * S5_PITCH + t_len, :] for i in range(S5_TILES)], axis=1)
        y = _dot(hcat.astype(BF16), cm_ref[...]) + d_ref[...] * u
        y = 0.5 * y * (1.0 + jnp.tanh(math.sqrt(2.0 / math.pi) * (y + 0.044715 * (y * y * y))))
        o_ref[b] = y * _sigmoid(_dot(y.astype(BF16), wg_ref[...]))


def _s5_matrices(lam_re, lam_im, b_re, b_im, c_re, c_im, log_dt):
    dt = jnp.exp(log_dt.astype(F32))[:, None]
    lr = lam_re.astype(F32)
    li = lam_im.astype(F32)
    mag = jnp.exp(lr * dt)
    a_re = mag * jnp.cos(li * dt)
    a_im = mag * jnp.sin(li * dt)
    den = lr * lr + li * li
    coef_re = ((a_re - 1.0) * lr + a_im * li) / den
    coef_im = (a_im * lr - (a_re - 1.0) * li) / den
    br = b_re.astype(F32)
    bi = b_im.astype(F32)
    bbar_re = coef_re[..., None] * br - coef_im[..., None] * bi
    bbar_im = coef_re[..., None] * bi + coef_im[..., None] * br
    eye = jnp.eye(S5_GROUPS, dtype=F32)
    n_state = S5_GROUPS * S5_STATE

    def b_block(bb):
        return jnp.einsum('gpc,gh->gchp', bb, eye).reshape(D_GROUP, n_state)

    def c_block(cc):
        return jnp.einsum('gcp,gh->gphc', cc, eye).reshape(n_state, D_GROUP)

    bm = jnp.concatenate([b_block(bbar_re), b_block(bbar_im)], axis=1).astype(BF16)
    cm = jnp.concatenate([c_block(c_re.astype(F32)), -c_block(c_im.astype(F32))], axis=0).astype(BF16)
    a = jnp.concatenate([a_re.reshape(n_state // LANES, LANES), a_im.reshape(n_state // LANES, LANES)], 0)
    return a, bm, cm


def _s5(z, a, bm, cm, d_skip, w_glu):
    bsz, s, _ = z.shape
    tb = min(S5_BLOCK, s)
    assert tb == S5_BLOCK
    const = lambda shape: pl.BlockSpec(shape, lambda i: (0, 0))
    n_state2 = 2 * S5_GROUPS * S5_STATE
    return pl.pallas_call(
        _s5_kernel,
        grid=(s // tb,),
        in_specs=[pl.BlockSpec((bsz, tb, D_GROUP), lambda i: (0, i, 6)),
                  const((S5_TILES, LANES)), const((D_GROUP, n_state2)), const((n_state2, D_GROUP)),
                  const((1, D_GROUP)), const((D_GROUP, D_GROUP))],
        out_specs=pl.BlockSpec((bsz, tb, D_GROUP), lambda i: (0, i, 0)),
        out_shape=jax.ShapeDtypeStruct((bsz, s, D_GROUP), F32),
        scratch_shapes=[pltpu.VMEM((bsz, S5_TILES * S5_PITCH, LANES), F32),
                        pltpu.VMEM((bsz, S5_TILES * S5_PITCH, LANES), F32),
                        pltpu.VMEM((bsz, S5_TILES, LANES), F32)],
        compiler_params=_cparams(("arbitrary",)),
        name="s5",
    )(z, a, bm, cm, d_skip.reshape(1, D_GROUP), w_glu)


def _t5_bucket_np(dist):
    max_exact = REL_BUCKETS // 2
    distf = np.maximum(dist, max_exact).astype(np.float32)
    large = max_exact + (np.log(distf / np.float32(max_exact)) / np.float32(math.log(REL_MAX_DIST / max_exact))
                         * np.float32(REL_BUCKETS - max_exact)).astype(np.int32)
    large = np.minimum(large, REL_BUCKETS - 1)
    return np.where(dist < max_exact, dist, large)


def _attn_bias(rel_bias):
    blk = ATT_BLOCK
    period = 3 * blk + 1
    k = np.arange(-(blk - 1), 2 * blk)
    tabs = []
    for window, dilation in ATT_PATTERNS:
        delta = blk - k
        valid = (delta >= 0) & (delta <= window // dilation)
        bucket = _t5_bucket_np(np.maximum(delta, 0) * dilation)
        vals = jnp.where(valid[:, None], rel_bias.astype(F32)[bucket] * LOG2E, NEG_BIG)
        v = jnp.full((period, ATT_HEADS), NEG_BIG, F32).at[k % period].set(vals).T
        tiled = jnp.tile(v, (1, blk))[:, :blk * (period - 1)]
        tabs.append(tiled.reshape(ATT_HEADS, blk, period - 1)[:, :, :2 * blk])
    return jnp.concatenate(tabs, axis=0)


def _attn_kernel(q_ref, k_ref, v_ref, qg_ref, kg_ref, bias_ref, o_ref,
                 qn_ref, kn_ref, vn_ref, q16_ref, k16_ref, v16_ref,
                 on_ref, lse_ref, on16_ref, lse16_ref, btab_ref, sc0_ref, sc1_ref):
    L = q_ref.shape[0]
    blk = ATT_BLOCK
    n_pat = len(ATT_PATTERNS)
    max_dil = ATT_PATTERNS[-1][1]
    tail = ATT_TAIL
    p16 = ATT_P16
    step = pl.program_id(1)
    first = step == 0
    cur = step % 2
    half_lane = lax.broadcasted_iota(jnp.int32, (blk, LANES), 1) < ATT_DH
    ones_blk = jnp.where(
        lax.broadcasted_iota(jnp.int32, (LANES, LANES), 0) // ATT_DH
        == lax.broadcasted_iota(jnp.int32, (LANES, LANES), 1) // ATT_DH, 1.0, 0.0).astype(BF16)

    @pl.when(first)
    def _():
        kn_ref[:, 0:tail, :] = jnp.zeros((2, tail, LANES), F32)
        vn_ref[:, 0:tail, :] = jnp.zeros((2, tail, LANES), F32)
        k16_ref[...] = jnp.zeros_like(k16_ref)
        v16_ref[...] = jnp.zeros_like(v16_ref)
        prev_cols = lax.broadcasted_iota(jnp.int32, (blk, 2 * blk), 1) < blk
        for t in range(n_pat * ATT_HEADS):
            btab_ref[0, t] = bias_ref[t]
            btab_ref[1, t] = jnp.where(prev_cols, NEG_BIG, bias_ref[t])

    @pl.when(step == 1)
    def _():
        for t in range(n_pat * ATT_HEADS):
            btab_ref[1, t] = bias_ref[t]

    @pl.when(jnp.logical_not(first))
    def _():
        kn_ref[:, 0:tail, :] = kn_ref[:, L:L + tail, :]
        vn_ref[:, 0:tail, :] = vn_ref[:, L:L + tail, :]

    def head_norm(x, g):
        ms = _dot((x * x).astype(BF16), ones_blk) * (1.0 / ATT_DH)
        return x * lax.rsqrt(ms + EPS) * g

    for p in range(2):
        ls = slice(p * LANES, (p + 1) * LANES)
        qn_ref[p] = head_norm(q_ref[:, ls].astype(F32), qg_ref[...]) * (ATT_DH ** -0.5 * LOG2E)
        kn_ref[p, tail:tail + L, :] = head_norm(k_ref[:, ls].astype(F32), kg_ref[...])
        vn_ref[p, tail:tail + L, :] = v_ref[:, ls].astype(F32)

    def regroup(n, carry):
        src = pl.multiple_of(n * max_dil, max_dil)
        dst = pl.ds(n, max_dil, stride=p16)
        for p in range(2):
            q16_ref[p, dst, :] = qn_ref[p, pl.ds(src, max_dil), :]
            k16_ref[p, cur, dst, :] = kn_ref[p, pl.ds(tail + src, max_dil), :]
            v16_ref[p, cur, dst, :] = vn_ref[p, pl.ds(tail + src, max_dil), :]
        return carry

    lax.fori_loop(0, L // max_dil, regroup, 0, unroll=4)

    def rows(start, n, dil):
        return pl.ds(start, n) if dil == 1 else pl.ds(start, n, stride=dil)

    def q_rows(ci, dil):
        return (ci % dil) + dil * blk * (ci // dil)

    def slab_rows(ci):
        return pl.ds(pl.multiple_of(ci * p16, SUBLANES), blk)

    def load_q(pi, p, ci):
        dil = ATT_PATTERNS[pi][1]
        if dil == max_dil:
            return q16_ref[p, slab_rows(ci), :]
        return qn_ref[p, rows(q_rows(ci, dil), blk, dil), :]

    def load_kv(nat_ref, slab_ref, pi, p, ci):
        dil = ATT_PATTERNS[pi][1]
        if dil == max_dil:
            return jnp.concatenate([slab_ref[p, 1 - cur, slab_rows(ci), :], slab_ref[p, cur, slab_rows(ci), :]],
                                   axis=0)
        return nat_ref[p, rows(tail + q_rows(ci, dil) - dil * blk, 2 * blk, dil), :]

    def store_out(pi, p, ci, on, lse):
        dil = ATT_PATTERNS[pi][1]
        if dil == max_dil:
            on16_ref[p, slab_rows(ci), :] = on
            lse16_ref[p, slab_rows(ci), :] = lse
        else:
            on_ref[pi, p, rows(q_rows(ci, dil), blk, dil), :] = on
            lse_ref[pi, p, rows(q_rows(ci, dil), blk, dil), :] = lse

    ones_cols = jnp.ones((2 * blk, LANES), BF16)
    n_combo = L // blk

    def scores(pi, ci, s_ref):
        tsel = jnp.where(ci < ATT_PATTERNS[pi][1], 1, 0)
        for p in range(2):
            qb = load_q(pi, p, ci)
            kb = load_kv(kn_ref, k16_ref, pi, p, ci).astype(BF16)
            for hh in range(2):
                sel = half_lane if hh == 0 else jnp.logical_not(half_lane)
                h = 2 * p + hh
                s_ref[h] = (_dot_nt(jnp.where(sel, qb, 0.0).astype(BF16), kb)
                            + btab_ref[tsel, pi * ATT_HEADS + h])

    def softmax_pv(pi, ci, s_ref):
        for p in range(2):
            vb = jnp.concatenate([load_kv(vn_ref, v16_ref, pi, p, ci).astype(BF16), ones_cols], axis=1)
            mxs, ovs = [], []
            for hh in range(2):
                sc = s_ref[2 * p + hh]
                mx = jnp.max(sc, axis=-1, keepdims=True)
                ovs.append(_dot(jnp.exp2(sc - mx).astype(BF16), vb))
                mxs.append(mx)
            den = jnp.where(half_lane, ovs[0][:, LANES:], ovs[1][:, LANES:])
            num = jnp.where(half_lane, ovs[0][:, :LANES], ovs[1][:, :LANES])
            store_out(pi, p, ci, num * (1.0 / den), jnp.where(half_lane, mxs[0], mxs[1]) + jnp.log2(den))

    scores(0, 0, sc0_ref)
    for pi in range(n_pat):
        def pair(j, carry, pi=pi):
            scores(pi, 2 * j + 1, sc1_ref)
            softmax_pv(pi, 2 * j, sc0_ref)
            scores(pi, 2 * j + 2, sc0_ref)
            softmax_pv(pi, 2 * j + 1, sc1_ref)
            return carry

        lax.fori_loop(0, n_combo // 2 - 1, pair, 0)
        scores(pi, n_combo - 1, sc1_ref)
        softmax_pv(pi, n_combo - 2, sc0_ref)
        if pi + 1 < n_pat:
            scores(pi + 1, 0, sc0_ref)
        softmax_pv(pi, n_combo - 1, sc1_ref)

    def merge(ti, carry):
        t0 = pl.multiple_of(ti * blk, blk)
        g0 = ti * (blk // max_dil)

        def regrouped(ref, p):
            return jnp.concatenate([ref[p, pl.ds(g0 + g, max_dil, stride=p16), :]
                                    for g in range(blk // max_dil)], axis=0)

        for p in range(2):
            ls_ = [lse_ref[pi, p, pl.ds(t0, blk), :] for pi in range(n_pat - 1)] + [regrouped(lse16_ref, p)]
            os_ = [on_ref[pi, p, pl.ds(t0, blk), :] for pi in range(n_pat - 1)] + [regrouped(on16_ref, p)]
            mx = functools.reduce(jnp.maximum, ls_)
            ws = [jnp.exp2(x - mx) for x in ls_]
            num = sum(w * o for w, o in zip(ws, os_))
            o_ref[pl.ds(t0, blk), p * LANES:(p + 1) * LANES] = num / sum(ws)
        return carry

    lax.fori_loop(0, L // blk, merge, 0)


def _attn(z, q_g, k_g, bias):
    bsz, s, _ = z.shape
    L = ATT_L
    assert s % L == 0
    gq = jnp.concatenate([q_g, q_g]).reshape(1, LANES).astype(F32)
    gk = jnp.concatenate([k_g, k_g]).reshape(1, LANES).astype(F32)
    n_pat = len(ATT_PATTERNS)
    slabs = ATT_PATTERNS[-1][1] * ATT_P16

    def zspec(j):
        return pl.BlockSpec((None, L, D_GROUP), lambda b, i, j=j: (b, i, j))

    return pl.pallas_call(
        _attn_kernel,
        grid=(bsz, s // L),
        in_specs=[zspec(7), zspec(8), zspec(9),
                  pl.BlockSpec((1, LANES), lambda b, i: (0, 0)),
                  pl.BlockSpec((1, LANES), lambda b, i: (0, 0)),
                  pl.BlockSpec(bias.shape, lambda b, i: (0, 0, 0))],
        out_specs=pl.BlockSpec((None, L, D_GROUP), lambda b, i: (b, i, 0)),
        out_shape=jax.ShapeDtypeStruct((bsz, s, D_GROUP), F32),
        scratch_shapes=[pltpu.VMEM((2, L, LANES), F32),
                        pltpu.VMEM((2, ATT_TAIL + L, LANES), F32),
                        pltpu.VMEM((2, ATT_TAIL + L, LANES), F32),
                        pltpu.VMEM((2, slabs, LANES), F32),
                        pltpu.VMEM((2, 2, slabs, LANES), F32),
                        pltpu.VMEM((2, 2, slabs, LANES), F32),
                        pltpu.VMEM((n_pat - 1, 2, L, LANES), F32),
                        pltpu.VMEM((n_pat - 1, 2, L, LANES), F32),
                        pltpu.VMEM((2, slabs, LANES), F32),
                        pltpu.VMEM((2, slabs, LANES), F32),
                        pltpu.VMEM((2,) + bias.shape, F32),
                        pltpu.VMEM((ATT_HEADS, ATT_BLOCK, 2 * ATT_BLOCK), F32),
                        pltpu.VMEM((ATT_HEADS, ATT_BLOCK, 2 * ATT_BLOCK), F32)],
        compiler_params=_cparams(("parallel", "arbitrary")),
        name="dilated_attn",
    )(z, z, z, gq, gk, bias)


def _post_kernel(x_ref, ya_ref, yc_ref, yd_ref, val0_ref, gate0_ref, valn_ref, gaten_ref,
                 cw_ref, cb_ref, clg_ref, clb_ref, gmix_ref, wout_ref, gmlp_ref, wup_ref, wdn_ref,
                 o_ref, h_ref, yb_ref):
    conv_params = (cw_ref, cb_ref, clg_ref, clb_ref)

    @pl.when(pl.program_id(1) == 0)
    def _():
        for step in _conv_tile_steps(val0_ref, gate0_ref, *conv_params, yb_ref, h_ref, first=True):
            step()

    conv_steps = _conv_tile_steps(valn_ref, gaten_ref, *conv_params, yb_ref, h_ref, first=False)
    ys = (ya_ref[...], yb_ref[...], yc_ref[...], yd_ref[...])
    conv_steps.pop(0)()
    acc = x_ref[...]
    for j, y in enumerate(ys):
        cs = slice(j * D_GROUP, (j + 1) * D_GROUP)
        yn = y * lax.rsqrt(jnp.mean(y * y, axis=-1, keepdims=True) + EPS) * gmix_ref[:, cs]
        acc = acc + _dot(yn.astype(BF16), wout_ref[cs, :])
    h = (acc * lax.rsqrt(jnp.mean(acc * acc, axis=-1, keepdims=True) + EPS) * gmlp_ref[...]).astype(BF16)
    mlp = jnp.zeros_like(acc)
    ff_tile = D_MODEL
    per_dot = len(conv_steps) * ff_tile // (2 * D_FF)
    for f0 in range(0, D_FF, ff_tile):
        hm = jnp.maximum(_dot(h, wup_ref[:, f0:f0 + ff_tile]), 0.0)
        for _ in range(per_dot):
            conv_steps.pop(0)()
        mlp = mlp + _dot((hm * hm).astype(BF16), wdn_ref[f0:f0 + ff_tile, :])
        for _ in range(per_dot):
            conv_steps.pop(0)()
    assert not conv_steps
    o_ref[...] = acc + mlp


def _post(x, z, y_a, y_c, y_d, conv_w, conv_b, ln_g, ln_b, gmix, gmlp, wout_all, wup_all, wdn_all, layer):
    bsz, s, d = x.shape
    n_tiles = s // ROW_TILE
    row = lambda n: pl.BlockSpec((None, ROW_TILE, n), lambda b, i: (b, i, 0))
    const = lambda shape: pl.BlockSpec(shape, lambda b, i: (0, 0), pipeline_mode=pl.Buffered(1))
    weight = lambda shape: pl.BlockSpec((None,) + shape, lambda b, i: (layer, 0, 0),
                                        pipeline_mode=pl.Buffered(1))
    z_first = lambda j: pl.BlockSpec((None, ROW_TILE, D_GROUP), lambda b, i: (b, 0, j))
    z_next = lambda j: pl.BlockSpec((None, ROW_TILE, D_GROUP),
                                    lambda b, i: (b, jnp.minimum(i + 1, n_tiles - 1), j))
    vec = lambda v: v.reshape(1, -1)
    return pl.pallas_call(
        _post_kernel,
        grid=(bsz, n_tiles),
        in_specs=[row(d), row(D_GROUP), row(D_GROUP), row(D_GROUP),
                  z_first(4), z_first(5), z_next(4), z_next(5),
                  const((CONV_WIDTH, D_GROUP)), const((1, D_GROUP)), const((1, D_GROUP)), const((1, D_GROUP)),
                  const((1, d)), weight((d, d)), const((1, d)), weight((d, D_FF)), weight((D_FF, d))],
        out_specs=row(d),
        out_shape=jax.ShapeDtypeStruct((bsz, s, d), F32),
        scratch_shapes=[pltpu.VMEM((CONV_HALO + ROW_TILE, D_GROUP), F32),
                        pltpu.VMEM((ROW_TILE, D_GROUP), F32)],
        compiler_params=_cparams(("parallel", "arbitrary")),
        name="outproj_mlp",
    )(x, y_a, y_c, y_d, z, z, z, z, conv_w, vec(conv_b), vec(ln_g), vec(ln_b),
      vec(gmix), wout_all, vec(gmlp), wup_all, wdn_all)


def kernel(x, norm_mix_g, w_in, hgrn_lb_logits, hgrn_norm_g, conv_w, conv_b, conv_ln_g, conv_ln_b, s5_lambda_re, s5_lambda_im, s5_b_re, s5_b_im, s5_c_re, s5_c_im, s5_d, s5_log_dt, s5_w_glu, attn_q_norm_g, attn_k_norm_g, rel_bias, mix_out_norm_g, w_out, norm_mlp_g, w_mlp_up, w_mlp_down):
    depth = w_in.shape[0]
    lb_sm = jax.nn.softmax(hgrn_lb_logits.astype(F32), axis=0)
    lb_all = jnp.maximum(jnp.cumsum(lb_sm, axis=0) - lb_sm[0], 0.0)
    bias = _attn_bias(rel_bias)
    w_in, w_out, w_mlp_up, w_mlp_down = (w.astype(BF16) for w in (w_in, w_out, w_mlp_up, w_mlp_down))
    for l in range(depth):
        z = _inproj(x, norm_mix_g[l], w_in, l)
        y_a = _hgrn(z, lb_all[l], hgrn_norm_g[l])
        a, bm, cm = _s5_matrices(s5_lambda_re[l], s5_lambda_im[l], s5_b_re[l], s5_b_im[l],
                                 s5_c_re[l], s5_c_im[l], s5_log_dt[l])
        y_c = _s5(z, a, bm, cm, s5_d[l], s5_w_glu[l].astype(BF16))
        y_d = _attn(z, attn_q_norm_g[l], attn_k_norm_g[l], bias)
        x = _post(x, z, y_a, y_c, y_d, conv_w[l], conv_b[l], conv_ln_g[l], conv_ln_b[l],
                  mix_out_norm_g[l], norm_mlp_g[l], w_out, w_mlp_up, w_mlp_down, l)
    return x
```

```python
import functools
import math

import numpy as np
import jax
import jax.numpy as jnp
from jax import lax
from jax.experimental import pallas as pl
from jax.experimental.pallas import tpu as pltpu

F32 = jnp.float32
BF16 = jnp.bfloat16

D_MODEL = 1024
D_GROUP = 256
N_MIXERS = 4
HG_HEADS = 4
HG_DK = 64
LB_FLOOR = 1e-30
CONV_WIDTH = 31
S5_CH = 16
S5_GROUPS = 16
S5_STATE = 64
ATT_HEADS = 4
ATT_DH = 64
ATT_PATTERNS = ((128, 1), (512, 4), (2048, 16))
ATT_BLOCK = 128
REL_BUCKETS = 32
REL_MAX_DIST = 2048
NEG_BIG = -1e30
D_FF = 4 * D_MODEL
N_IN_SLICES = 10
D_IN = N_IN_SLICES * D_GROUP
EPS = 1e-6
LOG2E = 1.4426950408889634

LANES = 128
SUBLANES = 8
VMEM_LIMIT = 56 * 1024 * 1024

ROW_TILE = 512
INPROJ_TILE = 1024
HG_CHUNK = 128
HG_BLOCK = 1024
CONV_SUB = 64
CONV_HALO = 32
S5_BLOCK = 256
S5_PITCH = S5_BLOCK + SUBLANES
S5_TILES = 2 * S5_GROUPS * S5_STATE // LANES
ATT_L = 2048
ATT_TAIL = 512
ATT_P16 = ATT_BLOCK + SUBLANES


def _cparams(sem):
    return pltpu.CompilerParams(dimension_semantics=sem, vmem_limit_bytes=VMEM_LIMIT)


def _sigmoid(x):
    return 1.0 / (1.0 + jnp.exp(-x))


def _dot(a, b):
    return jnp.dot(a, b, preferred_element_type=F32)


def _dot_nt(a, b):
    return lax.dot_general(a, b, (((1,), (1,)), ((), ())), preferred_element_type=F32)


def _dot_tn(a, b):
    return lax.dot_general(a, b, (((0,), (0,)), ((), ())), preferred_element_type=F32)


def _inproj_kernel(x_ref, g_ref, w_ref, z_ref):
    x = x_ref[...]
    h = x * lax.rsqrt(jnp.mean(x * x, axis=-1, keepdims=True) + EPS) * g_ref[...]
    z_ref[...] = _dot(h.astype(BF16), w_ref[...]).astype(z_ref.dtype)


def _inproj(x, g, w_all, layer):
    bsz, s, d = x.shape
    n = w_all.shape[2]
    return pl.pallas_call(
        _inproj_kernel,
        grid=(bsz, s // INPROJ_TILE),
        in_specs=[
            pl.BlockSpec((None, INPROJ_TILE, d), lambda b, i: (b, i, 0)),
            pl.BlockSpec((1, d), lambda b, i: (0, 0)),
            pl.BlockSpec((None, d, n), lambda b, i: (layer, 0, 0)),
        ],
        out_specs=pl.BlockSpec((None, INPROJ_TILE, n), lambda b, i: (b, i, 0)),
        out_shape=jax.ShapeDtypeStruct((bsz, s, n), BF16),
        compiler_params=_cparams(("parallel", "parallel")),
        name="inproj",
    )(x, g.reshape(1, d), w_all)


def _hold(gc, m):
    c, w = gc.shape
    if 2 * m >= SUBLANES:
        x = gc.reshape(c // (2 * m), 2 * m, w)
        return jnp.broadcast_to(x[:, m - 1:m, :], x.shape).reshape(c, w)
    x = gc.reshape(c // SUBLANES, SUBLANES, w)
    rows = lax.broadcasted_iota(jnp.int32, x.shape, 1)
    out = None
    for p in range(SUBLANES // (2 * m)):
        e = p * 2 * m + m - 1
        b = jnp.broadcast_to(x[:, e:e + 1, :], x.shape)
        out = b if out is None else jnp.where(rows >= p * 2 * m, b, out)
    return out.reshape(c, w)


def _hgrn_kernel(q_ref, f_ref, i_ref, g_ref, lb_ref, ng_ref, lvl_ref, o_ref, st_ref):
    @pl.when(pl.program_id(1) == 0)
    def _():
        st_ref[...] = jnp.zeros_like(st_ref)

    c = HG_CHUNK
    w = D_GROUP
    n_levels = int(math.log2(c))
    lane_head = lax.broadcasted_iota(jnp.int32, (c, w), 1) // HG_DK
    row = lax.broadcasted_iota(jnp.int32, (c, w), 0)
    blk = (lax.broadcasted_iota(jnp.int32, (w, w), 0) // HG_DK
           == lax.broadcasted_iota(jnp.int32, (w, w), 1) // HG_DK)
    ones_blk = jnp.where(blk, 1.0, 0.0).astype(BF16)
    lb = jnp.clip(lb_ref[...], 0.0, 1.0 - 1e-6)
    lb_floor = jnp.maximum(lb, LB_FLOOR)
    tri = jnp.where(lax.broadcasted_iota(jnp.int32, (c, c), 0) >= lax.broadcasted_iota(jnp.int32, (c, c), 1),
                    1.0, 0.0).astype(BF16)
    sgn = [jnp.where((row & (2 ** lv)) != 0, 1.0, -1.0) for lv in range(n_levels)]
    grp = SUBLANES

    def stack_heads(t):
        return jnp.concatenate(
            [jnp.where(lane_head == h, t, 0.0).astype(BF16) for h in range(HG_HEADS)], axis=0)

    def chunk(r0, st):
        q = q_ref[r0:r0 + c, :].astype(F32)
        fz = f_ref[r0:r0 + c, :].astype(F32)
        v = i_ref[r0:r0 + c, :].astype(F32)
        gate = g_ref[r0:r0 + c, :].astype(F32)
        gated = (1.0 - lb) * _sigmoid(fz)
        lf = jnp.log2(lb_floor + gated)
        kk = (1.0 - lb) - gated
        hi = lf.astype(BF16)
        r1 = lf - hi.astype(F32)
        mid = r1.astype(BF16)
        lo = (r1 - mid.astype(F32)).astype(BF16)
        gc = _dot(tri, hi) + _dot(tri, mid) + _dot(tri, lo)

        q_b = q.astype(BF16)
        k_b = [jnp.where(lane_head == h, kk, 0.0).astype(BF16) for h in range(HG_HEADS)]
        n_grp = c // grp
        scores = [None] * n_grp
        for lv in range(n_levels):
            m = 2 ** lv
            e = jnp.exp2((gc - _hold(gc, m)) * sgn[lv]).astype(BF16)
            p_lv = _dot_nt(q_b * e, jnp.concatenate([kb * e for kb in k_b], axis=0))
            for gi in range(n_grp):
                if m >= grp and (gi * grp) & m == 0:
                    continue
                rs = slice(gi * grp, (gi + 1) * grp)
                old = 0.0 if scores[gi] is None else scores[gi]
                scores[gi] = jnp.where(lvl_ref[rs, :] == lv, p_lv[rs], old)
        o = _dot(jnp.concatenate(scores, axis=0).astype(BF16), stack_heads(v))
        o = o + _dot((q * kk).astype(BF16), ones_blk) * v
        o = o + _dot_nt((q * jnp.exp2(gc)).astype(BF16), st.astype(BF16))
        g_last = gc[c - 1:c, :]
        upd = _dot_tn(v.astype(BF16), (kk * jnp.exp2(g_last - gc)).astype(BF16))
        st = st * jnp.exp2(g_last) + jnp.where(blk, upd, 0.0)

        ms = _dot((o * o).astype(BF16), ones_blk) * (1.0 / HG_DK)
        y = o * lax.rsqrt(ms + EPS) * ng_ref[...]
        o_ref[r0:r0 + c, :] = y * (gate * _sigmoid(gate))
        return st

    st = st_ref[...]
    for r0 in range(0, q_ref.shape[0], c):
        st = chunk(r0, st)
    st_ref[...] = st


def _hgrn_levels():
    t = np.arange(HG_CHUNK)[:, None]
    s = np.arange(HG_CHUNK)[None, :]
    x = np.maximum(t ^ s, 1)
    lv = np.where(s < t, np.floor(np.log2(x)).astype(np.int32), -1).astype(np.int32)
    return np.tile(lv, (1, HG_HEADS))


def _hgrn(z, lb, norm_g):
    bsz, s, _ = z.shape
    tb = min(HG_BLOCK, s)

    def zspec(j):
        return pl.BlockSpec((None, tb, D_GROUP), lambda b, i, j=j: (b, i, j))

    const = lambda shape: pl.BlockSpec(shape, lambda b, i: (0, 0))
    return pl.pallas_call(
        _hgrn_kernel,
        grid=(bsz, s // tb),
        in_specs=[zspec(0), zspec(1), zspec(2), zspec(3),
                  const((1, D_GROUP)), const((1, D_GROUP)), const((HG_CHUNK, HG_HEADS * HG_CHUNK))],
        out_specs=pl.BlockSpec((None, tb, D_GROUP), lambda b, i: (b, i, 0)),
        out_shape=jax.ShapeDtypeStruct((bsz, s, D_GROUP), F32),
        scratch_shapes=[pltpu.VMEM((D_GROUP, D_GROUP), F32)],
        compiler_params=_cparams(("parallel", "arbitrary")),
        name="hgrn2",
    )(z, z, z, z, lb.reshape(1, D_GROUP), norm_g.reshape(1, D_GROUP), jnp.asarray(_hgrn_levels()))


def _conv_tile_steps(val_ref, gate_ref, w_ref, b_ref, lg_ref, lbias_ref, o_ref, h_ref, first):
    tb = val_ref.shape[0]
    base = CONV_HALO - (CONV_WIDTH - 1)

    def glu():
        if first:
            h_ref[0:CONV_HALO, :] = jnp.zeros((CONV_HALO, D_GROUP), F32)
        else:
            h_ref[0:CONV_HALO, :] = h_ref[tb:tb + CONV_HALO, :]
        h_ref[CONV_HALO:CONV_HALO + tb, :] = val_ref[...].astype(F32) * _sigmoid(gate_ref[...].astype(F32))

    def sub_tile(t0):
        acc = jnp.broadcast_to(b_ref[...], (CONV_SUB, D_GROUP))
        for c in range(SUBLANES):
            part = None
            rows = CONV_SUB + (SUBLANES if c else 0)
            for j in range(CONV_WIDTH):
                if (base + j) % SUBLANES != c:
                    continue
                r0 = t0 + base + j - c
                term = w_ref[j:j + 1, :] * h_ref[r0:r0 + rows, :]
                part = term if part is None else part + term
            acc = acc + part[c:c + CONV_SUB, :]
        mu = jnp.mean(acc, axis=-1, keepdims=True)
        d = acc - mu
        var = jnp.mean(d * d, axis=-1, keepdims=True)
        y = d * lax.rsqrt(var + EPS) * lg_ref[...] + lbias_ref[...]
        o_ref[t0:t0 + CONV_SUB, :] = y * _sigmoid(y)

    return [glu] + [functools.partial(sub_tile, t0) for t0 in range(0, tb, CONV_SUB)]


def _s5_kernel(u_ref, a_ref, bm_ref, cm_ref, d_ref, wg_ref, o_ref, bu_ref, hs_ref, st_ref):
    nb, t_len, _ = u_ref.shape
    half = S5_TILES // 2

    @pl.when(pl.program_id(0) == 0)
    def _():
        st_ref[...] = jnp.zeros_like(st_ref)

    ar = a_ref[0:half, :]
    ai = a_ref[half:S5_TILES, :]

    for b in range(nb):
        bu = _dot(u_ref[b], bm_ref[...])
        for i in range(S5_TILES):
            bu_ref[b, i * S5_PITCH:i * S5_PITCH + t_len, :] = bu[:, i * LANES:(i + 1) * LANES]

    hs = [(st_ref[b, 0:half, :], st_ref[b, half:S5_TILES, :]) for b in range(nb)]
    for t in range(t_len):
        for b in range(nb):
            hr, hi = hs[b]
            br = bu_ref[b, pl.ds(t, half, stride=S5_PITCH), :]
            bi = bu_ref[b, pl.ds(half * S5_PITCH + t, half, stride=S5_PITCH), :]
            nr = ar * hr - ai * hi + br
            ni = ar * hi + ai * hr + bi
            hs_ref[b, pl.ds(t, half, stride=S5_PITCH), :] = nr
            hs_ref[b, pl.ds(half * S5_PITCH + t, half, stride=S5_PITCH), :] = ni
            hs[b] = (nr, ni)
    for b in range(nb):
        st_ref[b, 0:half, :] = hs[b][0]
        st_ref[b, half:S5_TILES, :] = hs[b][1]

    for b in range(nb):
        u = u_ref[b].astype(F32)
        hcat = jnp.concatenate(
            [hs_ref[b, i * S5_PITCH:i * S5_PITCH + t_len, :] for i in range(S5_TILES)], axis=1)
        y = _dot(hcat.astype(BF16), cm_ref[...]) + d_ref[...] * u
        y = 0.5 * y * (1.0 + jnp.tanh(math.sqrt(2.0 / math.pi) * (y + 0.044715 * (y * y * y))))
        o_ref[b] = y * _sigmoid(_dot(y.astype(BF16), wg_ref[...]))


def _s5_matrices(lam_re, lam_im, b_re, b_im, c_re, c_im, log_dt):
    dt = jnp.exp(log_dt.astype(F32))[:, None]
    lr = lam_re.astype(F32)
    li = lam_im.astype(F32)
    mag = jnp.exp(lr * dt)
    a_re = mag * jnp.cos(li * dt)
    a_im = mag * jnp.sin(li * dt)
    den = lr * lr + li * li
    coef_re = ((a_re - 1.0) * lr + a_im * li) / den
    coef_im = (a_im * lr - (a_re - 1.0) * li) / den
    br = b_re.astype(F32)
    bi = b_im.astype(F32)
    bbar_re = coef_re[..., None] * br - coef_im[..., None] * bi
    bbar_im = coef_re[..., None] * bi + coef_im[..., None] * br
    eye = jnp.eye(S5_GROUPS, dtype=F32)
    n_state = S5_GROUPS * S5_STATE

    def b_block(bb):
        return jnp.einsum('gpc,gh->gchp', bb, eye).reshape(D_GROUP, n_state)

    def c_block(cc):
        return jnp.einsum('gcp,gh->gphc', cc, eye).reshape(n_state, D_GROUP)

    bm = jnp.concatenate([b_block(bbar_re), b_block(bbar_im)], axis=1).astype(BF16)
    cm = jnp.concatenate([c_block(c_re.astype(F32)), -c_block(c_im.astype(F32))], axis=0).astype(BF16)
    a = jnp.concatenate([a_re.reshape(n_state // LANES, LANES), a_im.reshape(n_state // LANES, LANES)], 0)
    return a, bm, cm


def _s5(z, a, bm, cm, d_skip, w_glu):
    bsz, s, _ = z.shape
    tb = min(S5_BLOCK, s)
    assert tb == S5_BLOCK
    const = lambda shape: pl.BlockSpec(shape, lambda i: (0, 0))
    n_state2 = 2 * S5_GROUPS * S5_STATE
    return pl.pallas_call(
        _s5_kernel,
        grid=(s // tb,),
        in_specs=[pl.BlockSpec((bsz, tb, D_GROUP), lambda i: (0, i, 6)),
                  const((S5_TILES, LANES)), const((D_GROUP, n_state2)), const((n_state2, D_GROUP)),
                  const((1, D_GROUP)), const((D_GROUP, D_GROUP))],
        out_specs=pl.BlockSpec((bsz, tb, D_GROUP), lambda i: (0, i, 0)),
        out_shape=jax.ShapeDtypeStruct((bsz, s, D_GROUP), F32),
        scratch_shapes=[pltpu.VMEM((bsz, S5_TILES * S5_PITCH, LANES), F32),
                        pltpu.VMEM((bsz, S5_TILES * S5_PITCH, LANES), F32),
                        pltpu.VMEM((bsz, S5_TILES, LANES), F32)],
        compiler_params=_cparams(("arbitrary",)),
        name="s5",
    )(z, a, bm, cm, d_skip.reshape(1, D_GROUP), w_glu)


def _t5_bucket_np(dist):
    max_exact = REL_BUCKETS // 2
    distf = np.maximum(dist, max_exact).astype(np.float32)
    large = max_exact + (np.log(distf / np.float32(max_exact)) / np.float32(math.log(REL_MAX_DIST / max_exact))
                         * np.float32(REL_BUCKETS - max_exact)).astype(np.int32)
    large = np.minimum(large, REL_BUCKETS - 1)
    return np.where(dist < max_exact, dist, large)


def _attn_bias(rel_bias):
    blk = ATT_BLOCK
    period = 3 * blk + 1
    k = np.arange(-(blk - 1), 2 * blk)
    tabs = []
    for window, dilation in ATT_PATTERNS:
        delta = blk - k
        valid = (delta >= 0) & (delta <= window // dilation)
        bucket = _t5_bucket_np(np.maximum(delta, 0) * dilation)
        vals = jnp.where(valid[:, None], rel_bias.astype(F32)[bucket] * LOG2E, NEG_BIG)
        v = jnp.full((period, ATT_HEADS), NEG_BIG, F32).at[k % period].set(vals).T
        tiled = jnp.tile(v, (1, blk))[:, :blk * (period - 1)]
        tabs.append(tiled.reshape(ATT_HEADS, blk, period - 1)[:, :, :2 * blk])
    return jnp.concatenate(tabs, axis=0)


def _attn_kernel(q_ref, k_ref, v_ref, qg_ref, kg_ref, bias_ref, o_ref,
                 qn_ref, kn_ref, vn_ref, q16_ref, k16_ref, v16_ref,
                 on_ref, lse_ref, on16_ref, lse16_ref, btab_ref, sc0_ref, sc1_ref):
    L = q_ref.shape[0]
    blk = ATT_BLOCK
    n_pat = len(ATT_PATTERNS)
    max_dil = ATT_PATTERNS[-1][1]
    tail = ATT_TAIL
    p16 = ATT_P16
    step = pl.program_id(1)
    first = step == 0
    cur = step % 2
    half_lane = lax.broadcasted_iota(jnp.int32, (blk, LANES), 1) < ATT_DH
    mean_blk = jnp.where(
        lax.broadcasted_iota(jnp.int32, (LANES, LANES), 0) // ATT_DH
        == lax.broadcasted_iota(jnp.int32, (LANES, LANES), 1) // ATT_DH, 1.0 / ATT_DH, 0.0).astype(BF16)

    @pl.when(first)
    def _():
        kn_ref[:, 0:tail, :] = jnp.zeros((2, tail, LANES), F32)
        vn_ref[:, 0:tail, :] = jnp.zeros((2, tail, LANES), F32)
        k16_ref[...] = jnp.zeros_like(k16_ref)
        v16_ref[...] = jnp.zeros_like(v16_ref)
        prev_cols = lax.broadcasted_iota(jnp.int32, (blk, 2 * blk), 1) < blk
        for t in range(n_pat * ATT_HEADS):
            btab_ref[0, t] = bias_ref[t]
            btab_ref[1, t] = jnp.where(prev_cols, NEG_BIG, bias_ref[t])

    @pl.when(step == 1)
    def _():
        for t in range(n_pat * ATT_HEADS):
            btab_ref[1, t] = bias_ref[t]

    @pl.when(jnp.logical_not(first))
    def _():
        kn_ref[:, 0:tail, :] = kn_ref[:, L:L + tail, :]
        vn_ref[:, 0:tail, :] = vn_ref[:, L:L + tail, :]

    def head_norm(x, g):
        ms = _dot((x * x).astype(BF16), mean_blk)
        return x * (lax.rsqrt(ms + EPS) * g)

    q_gain = qg_ref[...] * (ATT_DH ** -0.5 * LOG2E)
    for p in range(2):
        ls = slice(p * LANES, (p + 1) * LANES)
        qn_ref[p] = head_norm(q_ref[:, ls].astype(F32), q_gain)
        kn_ref[p, tail:tail + L, :] = head_norm(k_ref[:, ls].astype(F32), kg_ref[...])
        vn_ref[p, tail:tail + L, :] = v_ref[:, ls].astype(F32)

    def regroup(n, carry):
        src = pl.multiple_of(n * max_dil, max_dil)
        dst = pl.ds(n, max_dil, stride=p16)
        for p in range(2):
            q16_ref[p, dst, :] = qn_ref[p, pl.ds(src, max_dil), :]
            k16_ref[p, cur, dst, :] = kn_ref[p, pl.ds(tail + src, max_dil), :]
            v16_ref[p, cur, dst, :] = vn_ref[p, pl.ds(tail + src, max_dil), :]
        return carry

    lax.fori_loop(0, L // max_dil, regroup, 0, unroll=4)

    def rows(start, n, dil):
        return pl.ds(start, n) if dil == 1 else pl.ds(start, n, stride=dil)

    def q_rows(ci, dil):
        return (ci % dil) + dil * blk * (ci // dil)

    def slab_rows(ci):
        return pl.ds(pl.multiple_of(ci * p16, SUBLANES), blk)

    def load_q(pi, p, ci):
        dil = ATT_PATTERNS[pi][1]
        if dil == max_dil:
            return q16_ref[p, slab_rows(ci), :]
        return qn_ref[p, rows(q_rows(ci, dil), blk, dil), :]

    def load_kv(nat_ref, slab_ref, pi, p, ci):
        dil = ATT_PATTERNS[pi][1]
        if dil == max_dil:
            return jnp.concatenate([slab_ref[p, 1 - cur, slab_rows(ci), :], slab_ref[p, cur, slab_rows(ci), :]],
                                   axis=0)
        return nat_ref[p, rows(tail + q_rows(ci, dil) - dil * blk, 2 * blk, dil), :]

    def store_out(pi, p, ci, on, lse):
        dil = ATT_PATTERNS[pi][1]
        if dil == max_dil:
            on16_ref[p, slab_rows(ci), :] = on
            lse16_ref[p, slab_rows(ci), :] = lse
        else:
            on_ref[pi, p, rows(q_rows(ci, dil), blk, dil), :] = on
            lse_ref[pi, p, rows(q_rows(ci, dil), blk, dil), :] = lse

    ones_cols = jnp.ones((2 * blk, LANES), BF16)
    n_combo = L // blk

    def scores(pi, ci, s_ref):
        tsel = jnp.where(ci < ATT_PATTERNS[pi][1], 1, 0)
        for p in range(2):
            qb = load_q(pi, p, ci)
            kb = load_kv(kn_ref, k16_ref, pi, p, ci).astype(BF16)
            for hh in range(2):
                sel = half_lane if hh == 0 else jnp.logical_not(half_lane)
                h = 2 * p + hh
                s_ref[h] = (_dot_nt(jnp.where(sel, qb, 0.0).astype(BF16), kb)
                            + btab_ref[tsel, pi * ATT_HEADS + h])

    def softmax_pv(pi, ci, s_ref):
        for p in range(2):
            vb = jnp.concatenate([load_kv(vn_ref, v16_ref, pi, p, ci).astype(BF16), ones_cols], axis=1)
            mxs, ovs = [], []
            for hh in range(2):
                sc = s_ref[2 * p + hh]
                mx = jnp.max(sc, axis=-1, keepdims=True)
                ovs.append(_dot(jnp.exp2(sc - mx).astype(BF16), vb))
                mxs.append(mx)
            den = jnp.where(half_lane, ovs[0][:, LANES:], ovs[1][:, LANES:])
            num = jnp.where(half_lane, ovs[0][:, :LANES], ovs[1][:, :LANES])
            store_out(pi, p, ci, num * (1.0 / den), jnp.where(half_lane, mxs[0], mxs[1]) + jnp.log2(den))

    scores(0, 0, sc0_ref)
    for pi in range(n_pat):
        def pair(j, carry, pi=pi):
            scores(pi, 2 * j + 1, sc1_ref)
            softmax_pv(pi, 2 * j, sc0_ref)
            scores(pi, 2 * j + 2, sc0_ref)
            softmax_pv(pi, 2 * j + 1, sc1_ref)
            return carry

        lax.fori_loop(0, n_combo // 2 - 1, pair, 0)
        scores(pi, n_combo - 1, sc1_ref)
        softmax_pv(pi, n_combo - 2, sc0_ref)
        if pi + 1 < n_pat:
            scores(pi + 1, 0, sc0_ref)
        softmax_pv(pi, n_combo - 1, sc1_ref)

    def merge(ti, carry):
        t0 = pl.multiple_of(ti * blk, blk)
        g0 = ti * (blk // max_dil)

        def regrouped(ref, p):
            return jnp.concatenate([ref[p, pl.ds(g0 + g, max_dil, stride=p16), :]
                                    for g in range(blk // max_dil)], axis=0)

        for p in range(2):
            ls_ = [lse_ref[pi, p, pl.ds(t0, blk), :] for pi in range(n_pat - 1)] + [regrouped(lse16_ref, p)]
            os_ = [on_ref[pi, p, pl.ds(t0, blk), :] for pi in range(n_pat - 1)] + [regrouped(on16_ref, p)]
            mx = functools.reduce(jnp.maximum, ls_)
            ws = [jnp.exp2(x - mx) for x in ls_]
            num = sum(w * o for w, o in zip(ws, os_))
            o_ref[pl.ds(t0, blk), p * LANES:(p + 1) * LANES] = num / sum(ws)
        return carry

    lax.fori_loop(0, L // blk, merge, 0)


def _attn(z, q_g, k_g, bias):
    bsz, s, _ = z.shape
    L = ATT_L
    assert s % L == 0
    gq = jnp.concatenate([q_g, q_g]).reshape(1, LANES).astype(F32)
    gk = jnp.concatenate([k_g, k_g]).reshape(1, LANES).astype(F32)
    n_pat = len(ATT_PATTERNS)
    slabs = ATT_PATTERNS[-1][1] * ATT_P16

    def zspec(j):
        return pl.BlockSpec((None, L, D_GROUP), lambda b, i, j=j: (b, i, j))

    return pl.pallas_call(
        _attn_kernel,
        grid=(bsz, s // L),
        in_specs=[zspec(7), zspec(8), zspec(9),
                  pl.BlockSpec((1, LANES), lambda b, i: (0, 0)),
                  pl.BlockSpec((1, LANES), lambda b, i: (0, 0)),
                  pl.BlockSpec(bias.shape, lambda b, i: (0, 0, 0))],
        out_specs=pl.BlockSpec((None, L, D_GROUP), lambda b, i: (b, i, 0)),
        out_shape=jax.ShapeDtypeStruct((bsz, s, D_GROUP), F32),
        scratch_shapes=[pltpu.VMEM((2, L, LANES), F32),
                        pltpu.VMEM((2, ATT_TAIL + L, LANES), F32),
                        pltpu.VMEM((2, ATT_TAIL + L, LANES), F32),
                        pltpu.VMEM((2, slabs, LANES), F32),
                        pltpu.VMEM((2, 2, slabs, LANES), F32),
                        pltpu.VMEM((2, 2, slabs, LANES), F32),
                        pltpu.VMEM((n_pat - 1, 2, L, LANES), F32),
                        pltpu.VMEM((n_pat - 1, 2, L, LANES), F32),
                        pltpu.VMEM((2, slabs, LANES), F32),
                        pltpu.VMEM((2, slabs, LANES), F32),
                        pltpu.VMEM((2,) + bias.shape, F32),
                        pltpu.VMEM((ATT_HEADS, ATT_BLOCK, 2 * ATT_BLOCK), F32),
                        pltpu.VMEM((ATT_HEADS, ATT_BLOCK, 2 * ATT_BLOCK), F32)],
        compiler_params=_cparams(("parallel", "arbitrary")),
        name="dilated_attn",
    )(z, z, z, gq, gk, bias)


def _post_kernel(x_ref, ya_ref, yc_ref, yd_ref, val0_ref, gate0_ref, valn_ref, gaten_ref,
                 cw_ref, cb_ref, clg_ref, clb_ref, gmix_ref, wout_ref, gmlp_ref, wup_ref, wdn_ref,
                 o_ref, h_ref, yb_ref):
    conv_params = (cw_ref, cb_ref, clg_ref, clb_ref)

    @pl.when(pl.program_id(1) == 0)
    def _():
        for step in _conv_tile_steps(val0_ref, gate0_ref, *conv_params, yb_ref, h_ref, first=True):
            step()

    conv_steps = _conv_tile_steps(valn_ref, gaten_ref, *conv_params, yb_ref, h_ref, first=False)
    ys = (ya_ref[...], yb_ref[...], yc_ref[...], yd_ref[...])
    conv_steps.pop(0)()
    acc = x_ref[...]
    for j, y in enumerate(ys):
        cs = slice(j * D_GROUP, (j + 1) * D_GROUP)
        yn = y * lax.rsqrt(jnp.mean(y * y, axis=-1, keepdims=True) + EPS) * gmix_ref[:, cs]
        acc = acc + _dot(yn.astype(BF16), wout_ref[cs, :])
    h = (acc * lax.rsqrt(jnp.mean(acc * acc, axis=-1, keepdims=True) + EPS) * gmlp_ref[...]).astype(BF16)
    mlp = jnp.zeros_like(acc)
    ff_tile = D_MODEL
    per_dot = len(conv_steps) * ff_tile // (2 * D_FF)
    for f0 in range(0, D_FF, ff_tile):
        hm = jnp.maximum(_dot(h, wup_ref[:, f0:f0 + ff_tile]), 0.0)
        for _ in range(per_dot):
            conv_steps.pop(0)()
        mlp = mlp + _dot((hm * hm).astype(BF16), wdn_ref[f0:f0 + ff_tile, :])
        for _ in range(per_dot):
            conv_steps.pop(0)()
    assert not conv_steps
    o_ref[...] = acc + mlp


def _post(x, z, y_a, y_c, y_d, conv_w, conv_b, ln_g, ln_b, gmix, gmlp, wout_all, wup_all, wdn_all, layer):
    bsz, s, d = x.shape
    n_tiles = s // ROW_TILE
    row = lambda n: pl.BlockSpec((None, ROW_TILE, n), lambda b, i: (b, i, 0))
    const = lambda shape: pl.BlockSpec(shape, lambda b, i: (0, 0), pipeline_mode=pl.Buffered(1))
    weight = lambda shape: pl.BlockSpec((None,) + shape, lambda b, i: (layer, 0, 0),
                                        pipeline_mode=pl.Buffered(1))
    z_first = lambda j: pl.BlockSpec((None, ROW_TILE, D_GROUP), lambda b, i: (b, 0, j))
    z_next = lambda j: pl.BlockSpec((None, ROW_TILE, D_GROUP),
                                    lambda b, i: (b, jnp.minimum(i + 1, n_tiles - 1), j))
    vec = lambda v: v.reshape(1, -1)
    return pl.pallas_call(
        _post_kernel,
        grid=(bsz, n_tiles),
        in_specs=[row(d), row(D_GROUP), row(D_GROUP), row(D_GROUP),
                  z_first(4), z_first(5), z_next(4), z_next(5),
                  const((CONV_WIDTH, D_GROUP)), const((1, D_GROUP)), const((1, D_GROUP)), const((1, D_GROUP)),
                  const((1, d)), weight((d, d)), const((1, d)), weight((d, D_FF)), weight((D_FF, d))],
        out_specs=row(d),
        out_shape=jax.ShapeDtypeStruct((bsz, s, d), F32),
        scratch_shapes=[pltpu.VMEM((CONV_HALO + ROW_TILE, D_GROUP), F32),
                        pltpu.VMEM((ROW_TILE, D_GROUP), F32)],
        compiler_params=_cparams(("parallel", "arbitrary")),
        name="outproj_mlp",
    )(x, y_a, y_c, y_d, z, z, z, z, conv_w, vec(conv_b), vec(ln_g), vec(ln_b),
      vec(gmix), wout_all, vec(gmlp), wup_all, wdn_all)


def kernel(x, norm_mix_g, w_in, hgrn_lb_logits, hgrn_norm_g, conv_w, conv_b, conv_ln_g, conv_ln_b, s5_lambda_re, s5_lambda_im, s5_b_re, s5_b_im, s5_c_re, s5_c_im, s5_d, s5_log_dt, s5_w_glu, attn_q_norm_g, attn_k_norm_g, rel_bias, mix_out_norm_g, w_out, norm_mlp_g, w_mlp_up, w_mlp_down):
    depth = w_in.shape[0]
    lb_sm = jax.nn.softmax(hgrn_lb_logits.astype(F32), axis=0)
    lb_all = jnp.maximum(jnp.cumsum(lb_sm, axis=0) - lb_sm[0], 0.0)
    bias = _attn_bias(rel_bias)
    w_in, w_out, w_mlp_up, w_mlp_down = (w.astype(BF16) for w in (w_in, w_out, w_mlp_up, w_mlp_down))
    for l in range(depth):
        z = _inproj(x, norm_mix_g[l], w_in, l)
        y_a = _hgrn(z, lb_all[l], hgrn_norm_g[l])
        a, bm, cm = _s5_matrices(s5_lambda_re[l], s5_lambda_im[l], s5_b_re[l], s5_b_im[l],
                                 s5_c_re[l], s5_c_im[l], s5_log_dt[l])
        y_c = _s5(z, a, bm, cm, s5_d[l], s5_w_glu[l].astype(BF16))
        y_d = _attn(z, attn_q_norm_g[l], attn_k_norm_g[l], bias)
        x = _post(x, z, y_a, y_c, y_d, conv_w[l], conv_b[l], conv_ln_g[l], conv_ln_b[l],
                  mix_out_norm_g[l], norm_mlp_g[l], w_out, w_mlp_up, w_mlp_down, l)
    return x
```

```python
import functools
import math

import numpy as np
import jax
import jax.numpy as jnp
from jax import lax
from jax.experimental import pallas as pl
from jax.experimental.pallas import tpu as pltpu

F32 = jnp.float32
BF16 = jnp.bfloat16

D_MODEL = 1024
D_GROUP = 256
N_MIXERS = 4
HG_HEADS = 4
HG_DK = 64
LB_FLOOR = 1e-30
CONV_WIDTH = 31
S5_CH = 16
S5_GROUPS = 16
S5_STATE = 64
ATT_HEADS = 4
ATT_DH = 64
ATT_PATTERNS = ((128, 1), (512, 4), (2048, 16))
ATT_BLOCK = 128
REL_BUCKETS = 32
REL_MAX_DIST = 2048
NEG_BIG = -1e30
D_FF = 4 * D_MODEL
N_IN_SLICES = 10
D_IN = N_IN_SLICES * D_GROUP
EPS = 1e-6
LOG2E = 1.4426950408889634

LANES = 128
SUBLANES = 8
VMEM_LIMIT = 56 * 1024 * 1024

ROW_TILE = 512
INPROJ_TILE = 1024
HG_CHUNK = 128
HG_BLOCK = 1024
CONV_SUB = 64
CONV_HALO = 32
S5_BLOCK = 256
S5_PITCH = S5_BLOCK + SUBLANES // 2
S5_TILES = 2 * S5_GROUPS * S5_STATE // LANES
ATT_L = 2048
ATT_TAIL = 512
ATT_P16 = ATT_BLOCK + SUBLANES


def _cparams(sem):
    return pltpu.CompilerParams(dimension_semantics=sem, vmem_limit_bytes=VMEM_LIMIT)


def _sigmoid(x):
    return 1.0 / (1.0 + jnp.exp(-x))


def _dot(a, b):
    return jnp.dot(a, b, preferred_element_type=F32)


def _dot_nt(a, b):
    return lax.dot_general(a, b, (((1,), (1,)), ((), ())), preferred_element_type=F32)


def _dot_tn(a, b):
    return lax.dot_general(a, b, (((0,), (0,)), ((), ())), preferred_element_type=F32)


def _inproj_kernel(x_ref, g_ref, w_ref, z_ref):
    x = x_ref[...]
    h = x * lax.rsqrt(jnp.mean(x * x, axis=-1, keepdims=True) + EPS) * g_ref[...]
    z_ref[...] = _dot(h.astype(BF16), w_ref[...]).astype(z_ref.dtype)


def _inproj(x, g, w_all, layer):
    bsz, s, d = x.shape
    n = w_all.shape[2]
    return pl.pallas_call(
        _inproj_kernel,
        grid=(bsz, s // INPROJ_TILE),
        in_specs=[
            pl.BlockSpec((None, INPROJ_TILE, d), lambda b, i: (b, i, 0)),
            pl.BlockSpec((1, d), lambda b, i: (0, 0)),
            pl.BlockSpec((None, d, n), lambda b, i: (layer, 0, 0)),
        ],
        out_specs=pl.BlockSpec((None, INPROJ_TILE, n), lambda b, i: (b, i, 0)),
        out_shape=jax.ShapeDtypeStruct((bsz, s, n), BF16),
        compiler_params=_cparams(("parallel", "parallel")),
        name="inproj",
    )(x, g.reshape(1, d), w_all)


def _hold(gc, m):
    c, w = gc.shape
    if 2 * m >= SUBLANES:
        x = gc.reshape(c // (2 * m), 2 * m, w)
        return jnp.broadcast_to(x[:, m - 1:m, :], x.shape).reshape(c, w)
    x = gc.reshape(c // SUBLANES, SUBLANES, w)
    rows = lax.broadcasted_iota(jnp.int32, x.shape, 1)
    out = None
    for p in range(SUBLANES // (2 * m)):
        e = p * 2 * m + m - 1
        b = jnp.broadcast_to(x[:, e:e + 1, :], x.shape)
        out = b if out is None else jnp.where(rows >= p * 2 * m, b, out)
    return out.reshape(c, w)


def _hgrn_kernel(q_ref, f_ref, i_ref, g_ref, lb_ref, ng_ref, lvl_ref, o_ref, st_ref):
    @pl.when(pl.program_id(1) == 0)
    def _():
        st_ref[...] = jnp.zeros_like(st_ref)

    c = HG_CHUNK
    w = D_GROUP
    n_levels = int(math.log2(c))
    lane_head = lax.broadcasted_iota(jnp.int32, (c, w), 1) // HG_DK
    row = lax.broadcasted_iota(jnp.int32, (c, w), 0)
    blk = (lax.broadcasted_iota(jnp.int32, (w, w), 0) // HG_DK
           == lax.broadcasted_iota(jnp.int32, (w, w), 1) // HG_DK)
    ones_blk = jnp.where(blk, 1.0, 0.0).astype(BF16)
    lb = jnp.clip(lb_ref[...], 0.0, 1.0 - 1e-6)
    lb_floor = jnp.maximum(lb, LB_FLOOR)
    tri = jnp.where(lax.broadcasted_iota(jnp.int32, (c, c), 0) >= lax.broadcasted_iota(jnp.int32, (c, c), 1),
                    1.0, 0.0).astype(BF16)
    sgn = [jnp.where((row & (2 ** lv)) != 0, 1.0, -1.0) for lv in range(n_levels)]
    grp = SUBLANES

    def stack_heads(t):
        return jnp.concatenate(
            [jnp.where(lane_head == h, t, 0.0).astype(BF16) for h in range(HG_HEADS)], axis=0)

    def chunk(r0, st):
        q = q_ref[r0:r0 + c, :].astype(F32)
        fz = f_ref[r0:r0 + c, :].astype(F32)
        v = i_ref[r0:r0 + c, :].astype(F32)
        gate = g_ref[r0:r0 + c, :].astype(F32)
        gated = (1.0 - lb) * _sigmoid(fz)
        lf = jnp.log2(lb_floor + gated)
        kk = (1.0 - lb) - gated
        hi = lf.astype(BF16)
        r1 = lf - hi.astype(F32)
        mid = r1.astype(BF16)
        lo = (r1 - mid.astype(F32)).astype(BF16)
        gc = _dot(tri, hi) + _dot(tri, mid) + _dot(tri, lo)

        q_b = q.astype(BF16)
        k_b = [jnp.where(lane_head == h, kk, 0.0).astype(BF16) for h in range(HG_HEADS)]
        n_grp = c // grp
        scores = [None] * n_grp
        for lv in range(n_levels):
            m = 2 ** lv
            e = jnp.exp2((gc - _hold(gc, m)) * sgn[lv]).astype(BF16)
            p_lv = _dot_nt(q_b * e, jnp.concatenate([kb * e for kb in k_b], axis=0))
            for gi in range(n_grp):
                if m >= grp and (gi * grp) & m == 0:
                    continue
                rs = slice(gi * grp, (gi + 1) * grp)
                old = 0.0 if scores[gi] is None else scores[gi]
                scores[gi] = jnp.where(lvl_ref[rs, :] == lv, p_lv[rs], old)
        o = _dot(jnp.concatenate(scores, axis=0).astype(BF16), stack_heads(v))
        qk = q * kk
        qk_hi = qk.astype(BF16)
        qk_lo = (qk - qk_hi.astype(F32)).astype(BF16)
        o = o + (_dot(qk_hi, ones_blk) + _dot(qk_lo, ones_blk)) * v
        o = o + _dot_nt((q * jnp.exp2(gc)).astype(BF16), st.astype(BF16))
        g_last = gc[c - 1:c, :]
        upd = _dot_tn(v.astype(BF16), (kk * jnp.exp2(g_last - gc)).astype(BF16))
        st = st * jnp.exp2(g_last) + jnp.where(blk, upd, 0.0)

        ms = _dot((o * o).astype(BF16), ones_blk) * (1.0 / HG_DK)
        y = o * lax.rsqrt(ms + EPS) * ng_ref[...]
        o_ref[r0:r0 + c, :] = y * (gate * _sigmoid(gate))
        return st

    st = st_ref[...]
    for r0 in range(0, q_ref.shape[0], c):
        st = chunk(r0, st)
    st_ref[...] = st


def _hgrn_levels():
    t = np.arange(HG_CHUNK)[:, None]
    s = np.arange(HG_CHUNK)[None, :]
    x = np.maximum(t ^ s, 1)
    lv = np.where(s < t, np.floor(np.log2(x)).astype(np.int32), -1).astype(np.int32)
    return np.tile(lv, (1, HG_HEADS))


def _hgrn(z, lb, norm_g):
    bsz, s, _ = z.shape
    tb = min(HG_BLOCK, s)

    def zspec(j):
        return pl.BlockSpec((None, tb, D_GROUP), lambda b, i, j=j: (b, i, j))

    const = lambda shape: pl.BlockSpec(shape, lambda b, i: (0, 0))
    return pl.pallas_call(
        _hgrn_kernel,
        grid=(bsz, s // tb),
        in_specs=[zspec(0), zspec(1), zspec(2), zspec(3),
                  const((1, D_GROUP)), const((1, D_GROUP)), const((HG_CHUNK, HG_HEADS * HG_CHUNK))],
        out_specs=pl.BlockSpec((None, tb, D_GROUP), lambda b, i: (b, i, 0)),
        out_shape=jax.ShapeDtypeStruct((bsz, s, D_GROUP), F32),
        scratch_shapes=[pltpu.VMEM((D_GROUP, D_GROUP), F32)],
        compiler_params=_cparams(("parallel", "arbitrary")),
        name="hgrn2",
    )(z, z, z, z, lb.reshape(1, D_GROUP), norm_g.reshape(1, D_GROUP), jnp.asarray(_hgrn_levels()))


def _conv_tile_steps(val_ref, gate_ref, w_ref, b_ref, lg_ref, lbias_ref, o_ref, h_ref, first):
    tb = val_ref.shape[0]
    base = CONV_HALO - (CONV_WIDTH - 1)

    def glu():
        if first:
            h_ref[0:CONV_HALO, :] = jnp.zeros((CONV_HALO, D_GROUP), F32)
        else:
            h_ref[0:CONV_HALO, :] = h_ref[tb:tb + CONV_HALO, :]
        h_ref[CONV_HALO:CONV_HALO + tb, :] = val_ref[...].astype(F32) * _sigmoid(gate_ref[...].astype(F32))

    def sub_tile(t0):
        acc = jnp.broadcast_to(b_ref[...], (CONV_SUB, D_GROUP))
        for c in range(SUBLANES):
            part = None
            rows = CONV_SUB + (SUBLANES if c else 0)
            for j in range(CONV_WIDTH):
                if (base + j) % SUBLANES != c:
                    continue
                r0 = t0 + base + j - c
                term = w_ref[j:j + 1, :] * h_ref[r0:r0 + rows, :]
                part = term if part is None else part + term
            acc = acc + part[c:c + CONV_SUB, :]
        mu = jnp.mean(acc, axis=-1, keepdims=True)
        d = acc - mu
        var = jnp.mean(d * d, axis=-1, keepdims=True)
        y = d * lax.rsqrt(var + EPS) * lg_ref[...] + lbias_ref[...]
        o_ref[t0:t0 + CONV_SUB, :] = y * _sigmoid(y)

    return [glu] + [functools.partial(sub_tile, t0) for t0 in range(0, tb, CONV_SUB)]


def _s5_kernel(u_ref, a_ref, bm_ref, cm_ref, d_ref, wg_ref, o_ref, bu_ref, hs_ref, st_ref):
    nb, t_len, _ = u_ref.shape
    half = S5_TILES // 2

    @pl.when(pl.program_id(0) == 0)
    def _():
        st_ref[...] = jnp.zeros_like(st_ref)

    ar = a_ref[0:half, :]
    ai = a_ref[half:S5_TILES, :]

    for b in range(nb):
        bu = _dot(u_ref[b], bm_ref[...])
        for i in range(S5_TILES):
            bu_ref[b, i * S5_PITCH:i * S5_PITCH + t_len, :] = bu[:, i * LANES:(i + 1) * LANES]

    hs = [(st_ref[b, 0:half, :], st_ref[b, half:S5_TILES, :]) for b in range(nb)]
    for t in range(t_len):
        for b in range(nb):
            hr, hi = hs[b]
            br = bu_ref[b, pl.ds(t, half, stride=S5_PITCH), :]
            bi = bu_ref[b, pl.ds(half * S5_PITCH + t, half, stride=S5_PITCH), :]
            nr = ar * hr - ai * hi + br
            ni = ar * hi + ai * hr + bi
            hs_ref[b, pl.ds(t, half, stride=S5_PITCH), :] = nr
            hs_ref[b, pl.ds(half * S5_PITCH + t, half, stride=S5_PITCH), :] = ni
            hs[b] = (nr, ni)
    for b in range(nb):
        st_ref[b, 0:half, :] = hs[b][0]
        st_ref[b, half:S5_TILES, :] = hs[b][1]

    for b in range(nb):
        u = u_ref[b].astype(F32)
        hcat = jnp.concatenate(
            [hs_ref[b, i * S5_PITCH:i * S5_PITCH + t_len, :] for i in range(S5_TILES)], axis=1)
        y = _dot(hcat.astype(BF16), cm_ref[...]) + d_ref[...] * u
        y = 0.5 * y * (1.0 + jnp.tanh(math.sqrt(2.0 / math.pi) * (y + 0.044715 * (y * y * y))))
        o_ref[b] = y * _sigmoid(_dot(y.astype(BF16), wg_ref[...]))


def _s5_matrices(lam_re, lam_im, b_re, b_im, c_re, c_im, log_dt):
    dt = jnp.exp(log_dt.astype(F32))[:, None]
    lr = lam_re.astype(F32)
    li = lam_im.astype(F32)
    mag = jnp.exp(lr * dt)
    a_re = mag * jnp.cos(li * dt)
    a_im = mag * jnp.sin(li * dt)
    den = lr * lr + li * li
    coef_re = ((a_re - 1.0) * lr + a_im * li) / den
    coef_im = (a_im * lr - (a_re - 1.0) * li) / den
    br = b_re.astype(F32)
    bi = b_im.astype(F32)
    bbar_re = coef_re[..., None] * br - coef_im[..., None] * bi
    bbar_im = coef_re[..., None] * bi + coef_im[..., None] * br
    eye = jnp.eye(S5_GROUPS, dtype=F32)
    n_state = S5_GROUPS * S5_STATE

    def b_block(bb):
        return jnp.einsum('gpc,gh->gchp', bb, eye).reshape(D_GROUP, n_state)

    def c_block(cc):
        return jnp.einsum('gcp,gh->gphc', cc, eye).reshape(n_state, D_GROUP)

    bm = jnp.concatenate([b_block(bbar_re), b_block(bbar_im)], axis=1).astype(BF16)
    cm = jnp.concatenate([c_block(c_re.astype(F32)), -c_block(c_im.astype(F32))], axis=0).astype(BF16)
    a = jnp.concatenate([a_re.reshape(n_state // LANES, LANES), a_im.reshape(n_state // LANES, LANES)], 0)
    return a, bm, cm


def _s5(z, a, bm, cm, d_skip, w_glu):
    bsz, s, _ = z.shape
    tb = min(S5_BLOCK, s)
    assert tb == S5_BLOCK
    const = lambda shape: pl.BlockSpec(shape, lambda i: (0, 0))
    n_state2 = 2 * S5_GROUPS * S5_STATE
    return pl.pallas_call(
        _s5_kernel,
        grid=(s // tb,),
        in_specs=[pl.BlockSpec((bsz, tb, D_GROUP), lambda i: (0, i, 6)),
                  const((S5_TILES, LANES)), const((D_GROUP, n_state2)), const((n_state2, D_GROUP)),
                  const((1, D_GROUP)), const((D_GROUP, D_GROUP))],
        out_specs=pl.BlockSpec((bsz, tb, D_GROUP), lambda i: (0, i, 0)),
        out_shape=jax.ShapeDtypeStruct((bsz, s, D_GROUP), F32),
        scratch_shapes=[pltpu.VMEM((bsz, S5_TILES * S5_PITCH, LANES), F32),
                        pltpu.VMEM((bsz, S5_TILES * S5_PITCH, LANES), F32),
                        pltpu.VMEM((bsz, S5_TILES, LANES), F32)],
        compiler_params=_cparams(("arbitrary",)),
        name="s5",
    )(z, a, bm, cm, d_skip.reshape(1, D_GROUP), w_glu)


def _t5_bucket_np(dist):
    max_exact = REL_BUCKETS // 2
    distf = np.maximum(dist, max_exact).astype(np.float32)
    large = max_exact + (np.log(distf / np.float32(max_exact)) / np.float32(math.log(REL_MAX_DIST / max_exact))
                         * np.float32(REL_BUCKETS - max_exact)).astype(np.int32)
    large = np.minimum(large, REL_BUCKETS - 1)
    return np.where(dist < max_exact, dist, large)


def _attn_bias(rel_bias):
    blk = ATT_BLOCK
    period = 3 * blk + 1
    k = np.arange(-(blk - 1), 2 * blk)
    tabs = []
    for window, dilation in ATT_PATTERNS:
        delta = blk - k
        valid = (delta >= 0) & (delta <= window // dilation)
        bucket = _t5_bucket_np(np.maximum(delta, 0) * dilation)
        vals = jnp.where(valid[:, None], rel_bias.astype(F32)[bucket] * LOG2E, NEG_BIG)
        v = jnp.full((period, ATT_HEADS), NEG_BIG, F32).at[k % period].set(vals).T
        tiled = jnp.tile(v, (1, blk))[:, :blk * (period - 1)]
        tabs.append(tiled.reshape(ATT_HEADS, blk, period - 1)[:, :, :2 * blk])
    return jnp.concatenate(tabs, axis=0)


def _attn_kernel(q_ref, k_ref, v_ref, qg_ref, kg_ref, bias_ref, o_ref,
                 qn_ref, kn_ref, vn_ref, q16_ref, k16_ref, v16_ref,
                 on_ref, lse_ref, on16_ref, lse16_ref, btab_ref, sc0_ref, sc1_ref):
    L = q_ref.shape[0]
    blk = ATT_BLOCK
    n_pat = len(ATT_PATTERNS)
    max_dil = ATT_PATTERNS[-1][1]
    tail = ATT_TAIL
    p16 = ATT_P16
    step = pl.program_id(1)
    first = step == 0
    cur = step % 2
    half_lane = lax.broadcasted_iota(jnp.int32, (blk, LANES), 1) < ATT_DH
    mean_blk = jnp.where(
        lax.broadcasted_iota(jnp.int32, (LANES, LANES), 0) // ATT_DH
        == lax.broadcasted_iota(jnp.int32, (LANES, LANES), 1) // ATT_DH, 1.0 / ATT_DH, 0.0).astype(BF16)

    @pl.when(first)
    def _():
        kn_ref[:, 0:tail, :] = jnp.zeros((2, tail, LANES), F32)
        vn_ref[:, 0:tail, :] = jnp.zeros((2, tail, LANES), F32)
        k16_ref[...] = jnp.zeros_like(k16_ref)
        v16_ref[...] = jnp.zeros_like(v16_ref)
        prev_cols = lax.broadcasted_iota(jnp.int32, (blk, 2 * blk), 1) < blk
        for t in range(n_pat * ATT_HEADS):
            btab_ref[0, t] = bias_ref[t]
            btab_ref[1, t] = jnp.where(prev_cols, NEG_BIG, bias_ref[t])

    @pl.when(step == 1)
    def _():
        for t in range(n_pat * ATT_HEADS):
            btab_ref[1, t] = bias_ref[t]

    @pl.when(jnp.logical_not(first))
    def _():
        kn_ref[:, 0:tail, :] = kn_ref[:, L:L + tail, :]
        vn_ref[:, 0:tail, :] = vn_ref[:, L:L + tail, :]

    def head_norm(x, g):
        ms = _dot((x * x).astype(BF16), mean_blk)
        return x * (lax.rsqrt(ms + EPS) * g)

    q_gain = qg_ref[...] * (ATT_DH ** -0.5 * LOG2E)
    for p in range(2):
        ls = slice(p * LANES, (p + 1) * LANES)
        qn_ref[p] = head_norm(q_ref[:, ls].astype(F32), q_gain)
        kn_ref[p, tail:tail + L, :] = head_norm(k_ref[:, ls].astype(F32), kg_ref[...])
        vn_ref[p, tail:tail + L, :] = v_ref[:, ls].astype(F32)

    def regroup(n, carry):
        src = pl.multiple_of(n * max_dil, max_dil)
        dst = pl.ds(n, max_dil, stride=p16)
        for p in range(2):
            q16_ref[p, dst, :] = qn_ref[p, pl.ds(src, max_dil), :]
            k16_ref[p, cur, dst, :] = kn_ref[p, pl.ds(tail + src, max_dil), :]
            v16_ref[p, cur, dst, :] = vn_ref[p, pl.ds(tail + src, max_dil), :]
        return carry

    lax.fori_loop(0, L // max_dil, regroup, 0, unroll=4)

    def rows(start, n, dil):
        return pl.ds(start, n) if dil == 1 else pl.ds(start, n, stride=dil)

    def q_rows(ci, dil):
        return (ci % dil) + dil * blk * (ci // dil)

    def slab_rows(ci):
        return pl.ds(pl.multiple_of(ci * p16, SUBLANES), blk)

    def load_q(pi, p, ci):
        dil = ATT_PATTERNS[pi][1]
        if dil == max_dil:
            return q16_ref[p, slab_rows(ci), :]
        return qn_ref[p, rows(q_rows(ci, dil), blk, dil), :]

    def load_kv(nat_ref, slab_ref, pi, p, ci):
        dil = ATT_PATTERNS[pi][1]
        if dil == max_dil:
            return jnp.concatenate([slab_ref[p, 1 - cur, slab_rows(ci), :], slab_ref[p, cur, slab_rows(ci), :]],
                                   axis=0)
        return nat_ref[p, rows(tail + q_rows(ci, dil) - dil * blk, 2 * blk, dil), :]

    def store_out(pi, p, ci, on, lse):
        dil = ATT_PATTERNS[pi][1]
        if dil == max_dil:
            on16_ref[p, slab_rows(ci), :] = on
            lse16_ref[p, slab_rows(ci), :] = lse
        else:
            on_ref[pi, p, rows(q_rows(ci, dil), blk, dil), :] = on
            lse_ref[pi, p, rows(q_rows(ci, dil), blk, dil), :] = lse

    ones_cols = jnp.ones((2 * blk, LANES), BF16)
    n_combo = L // blk

    def scores(pi, ci, s_ref):
        tsel = jnp.where(ci < ATT_PATTERNS[pi][1], 1, 0)
        for p in range(2):
            qb = load_q(pi, p, ci)
            kb = load_kv(kn_ref, k16_ref, pi, p, ci).astype(BF16)
            for hh in range(2):
                sel = half_lane if hh == 0 else jnp.logical_not(half_lane)
                h = 2 * p + hh
                s_ref[h] = (_dot_nt(jnp.where(sel, qb, 0.0).astype(BF16), kb)
                            + btab_ref[tsel, pi * ATT_HEADS + h])

    def softmax_pv(pi, ci, s_ref):
        for p in range(2):
            vb = jnp.concatenate([load_kv(vn_ref, v16_ref, pi, p, ci).astype(BF16), ones_cols], axis=1)
            mxs, ovs = [], []
            for hh in range(2):
                sc = s_ref[2 * p + hh]
                mx = jnp.max(sc, axis=-1, keepdims=True)
                ovs.append(_dot(jnp.exp2(sc - mx).astype(BF16), vb))
                mxs.append(mx)
            den = jnp.where(half_lane, ovs[0][:, LANES:], ovs[1][:, LANES:])
            num = jnp.where(half_lane, ovs[0][:, :LANES], ovs[1][:, :LANES])
            store_out(pi, p, ci, num * (1.0 / den), jnp.where(half_lane, mxs[0], mxs[1]) + jnp.log2(den))

    scores(0, 0, sc0_ref)
    for pi in range(n_pat):
        def pair(j, carry, pi=pi):
            scores(pi, 2 * j + 1, sc1_ref)
            softmax_pv(pi, 2 * j, sc0_ref)
            scores(pi, 2 * j + 2, sc0_ref)
            softmax_pv(pi, 2 * j + 1, sc1_ref)
            return carry

        lax.fori_loop(0, n_combo // 2 - 1, pair, 0)
        scores(pi, n_combo - 1, sc1_ref)
        softmax_pv(pi, n_combo - 2, sc0_ref)
        if pi + 1 < n_pat:
            scores(pi + 1, 0, sc0_ref)
        softmax_pv(pi, n_combo - 1, sc1_ref)

    def merge(ti, carry):
        t0 = pl.multiple_of(ti * blk, blk)
        g0 = ti * (blk // max_dil)

        def regrouped(ref, p):
            return jnp.concatenate([ref[p, pl.ds(g0 + g, max_dil, stride=p16), :]
                                    for g in range(blk // max_dil)], axis=0)

        for p in range(2):
            ls_ = [lse_ref[pi, p, pl.ds(t0, blk), :] for pi in range(n_pat - 1)] + [regrouped(lse16_ref, p)]
            os_ = [on_ref[pi, p, pl.ds(t0, blk), :] for pi in range(n_pat - 1)] + [regrouped(on16_ref, p)]
            mx = functools.reduce(jnp.maximum, ls_)
            ws = [jnp.exp2(x - mx) for x in ls_]
            num = sum(w * o for w, o in zip(ws, os_))
            o_ref[pl.ds(t0, blk), p * LANES:(p + 1) * LANES] = num / sum(ws)
        return carry

    lax.fori_loop(0, L // blk, merge, 0)


def _attn(z, q_g, k_g, bias):
    bsz, s, _ = z.shape
    L = ATT_L
    assert s % L == 0
    gq = jnp.concatenate([q_g, q_g]).reshape(1, LANES).astype(F32)
    gk = jnp.concatenate([k_g, k_g]).reshape(1, LANES).astype(F32)
    n_pat = len(ATT_PATTERNS)
    slabs = ATT_PATTERNS[-1][1] * ATT_P16

    def zspec(j):
        return pl.BlockSpec((None, L, D_GROUP), lambda b, i, j=j: (b, i, j))

    return pl.pallas_call(
        _attn_kernel,
        grid=(bsz, s // L),
        in_specs=[zspec(7), zspec(8), zspec(9),
                  pl.BlockSpec((1, LANES), lambda b, i: (0, 0)),
                  pl.BlockSpec((1, LANES), lambda b, i: (0, 0)),
                  pl.BlockSpec(bias.shape, lambda b, i: (0, 0, 0))],
        out_specs=pl.BlockSpec((None, L, D_GROUP), lambda b, i: (b, i, 0)),
        out_shape=jax.ShapeDtypeStruct((bsz, s, D_GROUP), F32),
        scratch_shapes=[pltpu.VMEM((2, L, LANES), F32),
                        pltpu.VMEM((2, ATT_TAIL + L, LANES), F32),
                        pltpu.VMEM((2, ATT_TAIL + L, LANES), F32),
                        pltpu.VMEM((2, slabs, LANES), F32),
                        pltpu.VMEM((2, 2, slabs, LANES), F32),
                        pltpu.VMEM((2, 2, slabs, LANES), F32),
                        pltpu.VMEM((n_pat - 1, 2, L, LANES), F32),
                        pltpu.VMEM((n_pat - 1, 2, L, LANES), F32),
                        pltpu.VMEM((2, slabs, LANES), F32),
                        pltpu.VMEM((2, slabs, LANES), F32),
                        pltpu.VMEM((2,) + bias.shape, F32),
                        pltpu.VMEM((ATT_HEADS, ATT_BLOCK, 2 * ATT_BLOCK), F32),
                        pltpu.VMEM((ATT_HEADS, ATT_BLOCK, 2 * ATT_BLOCK), F32)],
        compiler_params=_cparams(("parallel", "arbitrary")),
        name="dilated_attn",
    )(z, z, z, gq, gk, bias)


def _post_kernel(x_ref, ya_ref, yc_ref, yd_ref, val0_ref, gate0_ref, valn_ref, gaten_ref,
                 cw_ref, cb_ref, clg_ref, clb_ref, gmix_ref, wout_ref, gmlp_ref, wup_ref, wdn_ref,
                 o_ref, h_ref, yb_ref):
    conv_params = (cw_ref, cb_ref, clg_ref, clb_ref)

    @pl.when(pl.program_id(1) == 0)
    def _():
        for step in _conv_tile_steps(val0_ref, gate0_ref, *conv_params, yb_ref, h_ref, first=True):
            step()

    conv_steps = _conv_tile_steps(valn_ref, gaten_ref, *conv_params, yb_ref, h_ref, first=False)
    ys = (ya_ref[...], yb_ref[...], yc_ref[...], yd_ref[...])
    conv_steps.pop(0)()
    acc = x_ref[...]
    for j, y in enumerate(ys):
        cs = slice(j * D_GROUP, (j + 1) * D_GROUP)
        yn = y * lax.rsqrt(jnp.mean(y * y, axis=-1, keepdims=True) + EPS) * gmix_ref[:, cs]
        acc = acc + _dot(yn.astype(BF16), wout_ref[cs, :])
    h = (acc * lax.rsqrt(jnp.mean(acc * acc, axis=-1, keepdims=True) + EPS) * gmlp_ref[...]).astype(BF16)
    mlp = jnp.zeros_like(acc)
    ff_tile = D_MODEL
    per_dot = len(conv_steps) * ff_tile // (2 * D_FF)
    for f0 in range(0, D_FF, ff_tile):
        hm = jnp.maximum(_dot(h, wup_ref[:, f0:f0 + ff_tile]), 0.0)
        for _ in range(per_dot):
            conv_steps.pop(0)()
        mlp = mlp + _dot((hm * hm).astype(BF16), wdn_ref[f0:f0 + ff_tile, :])
        for _ in range(per_dot):
            conv_steps.pop(0)()
    assert not conv_steps
    o_ref[...] = acc + mlp


def _post(x, z, y_a, y_c, y_d, conv_w, conv_b, ln_g, ln_b, gmix, gmlp, wout_all, wup_all, wdn_all, layer):
    bsz, s, d = x.shape
    n_tiles = s // ROW_TILE
    row = lambda n: pl.BlockSpec((None, ROW_TILE, n), lambda b, i: (b, i, 0))
    const = lambda shape: pl.BlockSpec(shape, lambda b, i: (0, 0), pipeline_mode=pl.Buffered(1))
    weight = lambda shape: pl.BlockSpec((None,) + shape, lambda b, i: (layer, 0, 0),
                                        pipeline_mode=pl.Buffered(1))
    z_first = lambda j: pl.BlockSpec((None, ROW_TILE, D_GROUP), lambda b, i: (b, 0, j))
    z_next = lambda j: pl.BlockSpec((None, ROW_TILE, D_GROUP),
                                    lambda b, i: (b, jnp.minimum(i + 1, n_tiles - 1), j))
    vec = lambda v: v.reshape(1, -1)
    return pl.pallas_call(
        _post_kernel,
        grid=(bsz, n_tiles),
        in_specs=[row(d), row(D_GROUP), row(D_GROUP), row(D_GROUP),
                  z_first(4), z_first(5), z_next(4), z_next(5),
                  const((CONV_WIDTH, D_GROUP)), const((1, D_GROUP)), const((1, D_GROUP)), const((1, D_GROUP)),
                  const((1, d)), weight((d, d)), const((1, d)), weight((d, D_FF)), weight((D_FF, d))],
        out_specs=row(d),
        out_shape=jax.ShapeDtypeStruct((bsz, s, d), F32),
        scratch_shapes=[pltpu.VMEM((CONV_HALO + ROW_TILE, D_GROUP), F32),
                        pltpu.VMEM((ROW_TILE, D_GROUP), F32)],
        compiler_params=_cparams(("parallel", "arbitrary")),
        name="outproj_mlp",
    )(x, y_a, y_c, y_d, z, z, z, z, conv_w, vec(conv_b), vec(ln_g), vec(ln_b),
      vec(gmix), wout_all, vec(gmlp), wup_all, wdn_all)


def kernel(x, norm_mix_g, w_in, hgrn_lb_logits, hgrn_norm_g, conv_w, conv_b, conv_ln_g, conv_ln_b, s5_lambda_re, s5_lambda_im, s5_b_re, s5_b_im, s5_c_re, s5_c_im, s5_d, s5_log_dt, s5_w_glu, attn_q_norm_g, attn_k_norm_g, rel_bias, mix_out_norm_g, w_out, norm_mlp_g, w_mlp_up, w_mlp_down):
    depth = w_in.shape[0]
    lb_sm = jax.nn.softmax(hgrn_lb_logits.astype(F32), axis=0)
    lb_all = jnp.maximum(jnp.cumsum(lb_sm, axis=0) - lb_sm[0], 0.0)
    bias = _attn_bias(rel_bias)
    w_in, w_out, w_mlp_up, w_mlp_down = (w.astype(BF16) for w in (w_in, w_out, w_mlp_up, w_mlp_down))
    for l in range(depth):
        z = _inproj(x, norm_mix_g[l], w_in, l)
        y_a = _hgrn(z, lb_all[l], hgrn_norm_g[l])
        a, bm, cm = _s5_matrices(s5_lambda_re[l], s5_lambda_im[l], s5_b_re[l], s5_b_im[l],
                                 s5_c_re[l], s5_c_im[l], s5_log_dt[l])
        y_c = _s5(z, a, bm, cm, s5_d[l], s5_w_glu[l].astype(BF16))
        y_d = _attn(z, attn_q_norm_g[l], attn_k_norm_g[l], bias)
        x = _post(x, z, y_a, y_c, y_d, conv_w[l], conv_b[l], conv_ln_g[l], conv_ln_b[l],
                  mix_out_norm_g[l], norm_mlp_g[l], w_out, w_mlp_up, w_mlp_down, l)
    return x
```

```python
import functools
import math

import numpy as np
import jax
import jax.numpy as jnp
from jax import lax
from jax.experimental import pallas as pl
from jax.experimental.pallas import tpu as pltpu

F32 = jnp.float32
BF16 = jnp.bfloat16

D_MODEL = 1024
D_GROUP = 256
N_MIXERS = 4
HG_HEADS = 4
HG_DK = 64
LB_FLOOR = 1e-30
CONV_WIDTH = 31
S5_CH = 16
S5_GROUPS = 16
S5_STATE = 64
ATT_HEADS = 4
ATT_DH = 64
ATT_PATTERNS = ((128, 1), (512, 4), (2048, 16))
ATT_BLOCK = 128
REL_BUCKETS = 32
REL_MAX_DIST = 2048
NEG_BIG = -1e30
D_FF = 4 * D_MODEL
N_IN_SLICES = 10
D_IN = N_IN_SLICES * D_GROUP
EPS = 1e-6
LOG2E = 1.4426950408889634

LANES = 128
SUBLANES = 8
VMEM_LIMIT = 56 * 1024 * 1024

ROW_TILE = 512
INPROJ_TILE = 1024
HG_CHUNK = 128
HG_BLOCK = 1024
CONV_SUB = 64
CONV_HALO = 32
S5_BLOCK = 256
S5_PITCH = S5_BLOCK + SUBLANES // 2
S5_TILES = 2 * S5_GROUPS * S5_STATE // LANES
ATT_L = 2048
ATT_TAIL = 512
ATT_P16 = ATT_BLOCK + SUBLANES
ATT_UNROLL = 8


def _cparams(sem):
    return pltpu.CompilerParams(dimension_semantics=sem, vmem_limit_bytes=VMEM_LIMIT)


def _sigmoid(x):
    return 1.0 / (1.0 + jnp.exp(-x))


def _dot(a, b):
    return jnp.dot(a, b, preferred_element_type=F32)


def _dot_nt(a, b):
    return lax.dot_general(a, b, (((1,), (1,)), ((), ())), preferred_element_type=F32)


def _dot_tn(a, b):
    return lax.dot_general(a, b, (((0,), (0,)), ((), ())), preferred_element_type=F32)


def _inproj_kernel(x_ref, g_ref, w_ref, z_ref):
    x = x_ref[...]
    h = x * lax.rsqrt(jnp.mean(x * x, axis=-1, keepdims=True) + EPS) * g_ref[...]
    z_ref[...] = _dot(h.astype(BF16), w_ref[...]).astype(z_ref.dtype)


def _inproj(x, g, w_all, layer):
    bsz, s, d = x.shape
    n = w_all.shape[2]
    return pl.pallas_call(
        _inproj_kernel,
        grid=(bsz, s // INPROJ_TILE),
        in_specs=[
            pl.BlockSpec((None, INPROJ_TILE, d), lambda b, i: (b, i, 0)),
            pl.BlockSpec((1, d), lambda b, i: (0, 0)),
            pl.BlockSpec((None, d, n), lambda b, i: (layer, 0, 0)),
        ],
        out_specs=pl.BlockSpec((None, INPROJ_TILE, n), lambda b, i: (b, i, 0)),
        out_shape=jax.ShapeDtypeStruct((bsz, s, n), BF16),
        compiler_params=_cparams(("parallel", "parallel")),
        name="inproj",
    )(x, g.reshape(1, d), w_all)


def _hold(gc, m):
    c, w = gc.shape
    if 2 * m >= SUBLANES:
        x = gc.reshape(c // (2 * m), 2 * m, w)
        return jnp.broadcast_to(x[:, m - 1:m, :], x.shape).reshape(c, w)
    x = gc.reshape(c // SUBLANES, SUBLANES, w)
    rows = lax.broadcasted_iota(jnp.int32, x.shape, 1)
    out = None
    for p in range(SUBLANES // (2 * m)):
        e = p * 2 * m + m - 1
        b = jnp.broadcast_to(x[:, e:e + 1, :], x.shape)
        out = b if out is None else jnp.where(rows >= p * 2 * m, b, out)
    return out.reshape(c, w)


def _hgrn_kernel(q_ref, f_ref, i_ref, g_ref, lb_ref, ng_ref, lvl_ref, o_ref, st_ref):
    @pl.when(pl.program_id(1) == 0)
    def _():
        st_ref[...] = jnp.zeros_like(st_ref)

    c = HG_CHUNK
    w = D_GROUP
    n_levels = int(math.log2(c))
    lane_head = lax.broadcasted_iota(jnp.int32, (c, w), 1) // HG_DK
    row = lax.broadcasted_iota(jnp.int32, (c, w), 0)
    blk = (lax.broadcasted_iota(jnp.int32, (w, w), 0) // HG_DK
           == lax.broadcasted_iota(jnp.int32, (w, w), 1) // HG_DK)
    ones_blk = jnp.where(blk, 1.0, 0.0).astype(BF16)
    lb = jnp.clip(lb_ref[...], 0.0, 1.0 - 1e-6)
    lb_floor = jnp.maximum(lb, LB_FLOOR)
    tri = jnp.where(lax.broadcasted_iota(jnp.int32, (c, c), 0) >= lax.broadcasted_iota(jnp.int32, (c, c), 1),
                    1.0, 0.0).astype(BF16)
    sgn = [jnp.where((row & (2 ** lv)) != 0, 1.0, -1.0) for lv in range(n_levels)]
    grp = SUBLANES

    def stack_heads(t):
        return jnp.concatenate(
            [jnp.where(lane_head == h, t, 0.0).astype(BF16) for h in range(HG_HEADS)], axis=0)

    def chunk(r0, st):
        q = q_ref[r0:r0 + c, :].astype(F32)
        fz = f_ref[r0:r0 + c, :].astype(F32)
        v = i_ref[r0:r0 + c, :].astype(F32)
        gate = g_ref[r0:r0 + c, :].astype(F32)
        gated = (1.0 - lb) * _sigmoid(fz)
        lf = jnp.log2(lb_floor + gated)
        kk = (1.0 - lb) - gated
        hi = lf.astype(BF16)
        r1 = lf - hi.astype(F32)
        mid = r1.astype(BF16)
        lo = (r1 - mid.astype(F32)).astype(BF16)
        gc = _dot(tri, hi) + _dot(tri, mid) + _dot(tri, lo)

        q_b = q.astype(BF16)
        k_b = [jnp.where(lane_head == h, kk, 0.0).astype(BF16) for h in range(HG_HEADS)]
        n_grp = c // grp
        scores = [None] * n_grp
        for lv in range(n_levels):
            m = 2 ** lv
            e = jnp.exp2((gc - _hold(gc, m)) * sgn[lv]).astype(BF16)
            p_lv = _dot_nt(q_b * e, jnp.concatenate([kb * e for kb in k_b], axis=0))
            for gi in range(n_grp):
                if m >= grp and (gi * grp) & m == 0:
                    continue
                rs = slice(gi * grp, (gi + 1) * grp)
                old = 0.0 if scores[gi] is None else scores[gi]
                scores[gi] = jnp.where(lvl_ref[rs, :] == lv, p_lv[rs], old)
        o = _dot(jnp.concatenate(scores, axis=0).astype(BF16), stack_heads(v))
        qk = q * kk
        qk_hi = qk.astype(BF16)
        qk_lo = (qk - qk_hi.astype(F32)).astype(BF16)
        o = o + (_dot(qk_hi, ones_blk) + _dot(qk_lo, ones_blk)) * v
        o = o + _dot_nt((q * jnp.exp2(gc)).astype(BF16), st.astype(BF16))
        g_last = gc[c - 1:c, :]
        upd = _dot_tn(v.astype(BF16), (kk * jnp.exp2(g_last - gc)).astype(BF16))
        st = st * jnp.exp2(g_last) + jnp.where(blk, upd, 0.0)

        ms = _dot((o * o).astype(BF16), ones_blk) * (1.0 / HG_DK)
        y = o * lax.rsqrt(ms + EPS) * ng_ref[...]
        o_ref[r0:r0 + c, :] = y * (gate * _sigmoid(gate))
        return st

    st = st_ref[...]
    for r0 in range(0, q_ref.shape[0], c):
        st = chunk(r0, st)
    st_ref[...] = st


def _hgrn_levels():
    t = np.arange(HG_CHUNK)[:, None]
    s = np.arange(HG_CHUNK)[None, :]
    x = np.maximum(t ^ s, 1)
    lv = np.where(s < t, np.floor(np.log2(x)).astype(np.int32), -1).astype(np.int32)
    return np.tile(lv, (1, HG_HEADS))


def _hgrn(z, lb, norm_g):
    bsz, s, _ = z.shape
    tb = min(HG_BLOCK, s)

    def zspec(j):
        return pl.BlockSpec((None, tb, D_GROUP), lambda b, i, j=j: (b, i, j))

    const = lambda shape: pl.BlockSpec(shape, lambda b, i: (0, 0))
    return pl.pallas_call(
        _hgrn_kernel,
        grid=(bsz, s // tb),
        in_specs=[zspec(0), zspec(1), zspec(2), zspec(3),
                  const((1, D_GROUP)), const((1, D_GROUP)), const((HG_CHUNK, HG_HEADS * HG_CHUNK))],
        out_specs=pl.BlockSpec((None, tb, D_GROUP), lambda b, i: (b, i, 0)),
        out_shape=jax.ShapeDtypeStruct((bsz, s, D_GROUP), F32),
        scratch_shapes=[pltpu.VMEM((D_GROUP, D_GROUP), F32)],
        compiler_params=_cparams(("parallel", "arbitrary")),
        name="hgrn2",
    )(z, z, z, z, lb.reshape(1, D_GROUP), norm_g.reshape(1, D_GROUP), jnp.asarray(_hgrn_levels()))


def _conv_tile_steps(val_ref, gate_ref, w_ref, b_ref, lg_ref, lbias_ref, o_ref, h_ref, first):
    tb = val_ref.shape[0]
    base = CONV_HALO - (CONV_WIDTH - 1)

    def glu():
        if first:
            h_ref[0:CONV_HALO, :] = jnp.zeros((CONV_HALO, D_GROUP), F32)
        else:
            h_ref[0:CONV_HALO, :] = h_ref[tb:tb + CONV_HALO, :]
        h_ref[CONV_HALO:CONV_HALO + tb, :] = val_ref[...].astype(F32) * _sigmoid(gate_ref[...].astype(F32))

    def sub_tile(t0):
        acc = jnp.broadcast_to(b_ref[...], (CONV_SUB, D_GROUP))
        for c in range(SUBLANES):
            part = None
            rows = CONV_SUB + (SUBLANES if c else 0)
            for j in range(CONV_WIDTH):
                if (base + j) % SUBLANES != c:
                    continue
                r0 = t0 + base + j - c
                term = w_ref[j:j + 1, :] * h_ref[r0:r0 + rows, :]
                part = term if part is None else part + term
            acc = acc + part[c:c + CONV_SUB, :]
        mu = jnp.mean(acc, axis=-1, keepdims=True)
        d = acc - mu
        var = jnp.mean(d * d, axis=-1, keepdims=True)
        y = d * lax.rsqrt(var + EPS) * lg_ref[...] + lbias_ref[...]
        o_ref[t0:t0 + CONV_SUB, :] = y * _sigmoid(y)

    return [glu] + [functools.partial(sub_tile, t0) for t0 in range(0, tb, CONV_SUB)]


def _s5_kernel(u_ref, a_ref, bm_ref, cm_ref, d_ref, wg_ref, o_ref, bu_ref, hs_ref, st_ref):
    nb, t_len, _ = u_ref.shape
    half = S5_TILES // 2

    @pl.when(pl.program_id(0) == 0)
    def _():
        st_ref[...] = jnp.zeros_like(st_ref)

    ar = a_ref[0:half, :]
    ai = a_ref[half:S5_TILES, :]

    for b in range(nb):
        bu = _dot(u_ref[b], bm_ref[...])
        for i in range(S5_TILES):
            bu_ref[b, i * S5_PITCH:i * S5_PITCH + t_len, :] = bu[:, i * LANES:(i + 1) * LANES]

    hs = [(st_ref[b, 0:half, :], st_ref[b, half:S5_TILES, :]) for b in range(nb)]
    for t in range(t_len):
        for b in range(nb):
            hr, hi = hs[b]
            br = bu_ref[b, pl.ds(t, half, stride=S5_PITCH), :]
            bi = bu_ref[b, pl.ds(half * S5_PITCH + t, half, stride=S5_PITCH), :]
            nr = ar * hr - ai * hi + br
            ni = ar * hi + ai * hr + bi
            hs_ref[b, pl.ds(t, half, stride=S5_PITCH), :] = nr
            hs_ref[b, pl.ds(half * S5_PITCH + t, half, stride=S5_PITCH), :] = ni
            hs[b] = (nr, ni)
    for b in range(nb):
        st_ref[b, 0:half, :] = hs[b][0]
        st_ref[b, half:S5_TILES, :] = hs[b][1]

    ch = []
    for b in range(nb):
        hcat = jnp.concatenate(
            [hs_ref[b, i * S5_PITCH:i * S5_PITCH + t_len, :].astype(BF16) for i in range(S5_TILES)], axis=1)
        ch.append(_dot(hcat, cm_ref[...]))
    for b in range(nb):
        y = ch[b] + d_ref[...] * u_ref[b].astype(F32)
        y = 0.5 * y * (1.0 + jnp.tanh(math.sqrt(2.0 / math.pi) * (y + 0.044715 * (y * y * y))))
        o_ref[b] = y * _sigmoid(_dot(y.astype(BF16), wg_ref[...]))


def _s5_matrices(lam_re, lam_im, b_re, b_im, c_re, c_im, log_dt):
    dt = jnp.exp(log_dt.astype(F32))[:, None]
    lr = lam_re.astype(F32)
    li = lam_im.astype(F32)
    mag = jnp.exp(lr * dt)
    a_re = mag * jnp.cos(li * dt)
    a_im = mag * jnp.sin(li * dt)
    den = lr * lr + li * li
    coef_re = ((a_re - 1.0) * lr + a_im * li) / den
    coef_im = (a_im * lr - (a_re - 1.0) * li) / den
    br = b_re.astype(F32)
    bi = b_im.astype(F32)
    bbar_re = coef_re[..., None] * br - coef_im[..., None] * bi
    bbar_im = coef_re[..., None] * bi + coef_im[..., None] * br
    eye = jnp.eye(S5_GROUPS, dtype=F32)
    n_state = S5_GROUPS * S5_STATE

    def b_block(bb):
        return jnp.einsum('gpc,gh->gchp', bb, eye).reshape(D_GROUP, n_state)

    def c_block(cc):
        return jnp.einsum('gcp,gh->gphc', cc, eye).reshape(n_state, D_GROUP)

    bm = jnp.concatenate([b_block(bbar_re), b_block(bbar_im)], axis=1).astype(BF16)
    cm = jnp.concatenate([c_block(c_re.astype(F32)), -c_block(c_im.astype(F32))], axis=0).astype(BF16)
    a = jnp.concatenate([a_re.reshape(n_state // LANES, LANES), a_im.reshape(n_state // LANES, LANES)], 0)
    return a, bm, cm


def _s5(z, a_all, bm_all, cm_all, d_skip, wglu_all, layer):
    bsz, s, _ = z.shape
    tb = min(S5_BLOCK, s)
    assert tb == S5_BLOCK
    const = lambda shape: pl.BlockSpec(shape, lambda i: (0, 0))
    per_layer = lambda shape: pl.BlockSpec((None,) + shape, lambda i: (layer, 0, 0))
    n_state2 = 2 * S5_GROUPS * S5_STATE
    return pl.pallas_call(
        _s5_kernel,
        grid=(s // tb,),
        in_specs=[pl.BlockSpec((bsz, tb, D_GROUP), lambda i: (0, i, 6)),
                  per_layer((S5_TILES, LANES)), per_layer((D_GROUP, n_state2)), per_layer((n_state2, D_GROUP)),
                  const((1, D_GROUP)), per_layer((D_GROUP, D_GROUP))],
        out_specs=pl.BlockSpec((bsz, tb, D_GROUP), lambda i: (0, i, 0)),
        out_shape=jax.ShapeDtypeStruct((bsz, s, D_GROUP), F32),
        scratch_shapes=[pltpu.VMEM((bsz, S5_TILES * S5_PITCH, LANES), F32),
                        pltpu.VMEM((bsz, S5_TILES * S5_PITCH, LANES), F32),
                        pltpu.VMEM((bsz, S5_TILES, LANES), F32)],
        compiler_params=_cparams(("arbitrary",)),
        name="s5",
    )(z, a_all, bm_all, cm_all, d_skip.reshape(1, D_GROUP), wglu_all)


def _t5_bucket_np(dist):
    max_exact = REL_BUCKETS // 2
    distf = np.maximum(dist, max_exact).astype(np.float32)
    large = max_exact + (np.log(distf / np.float32(max_exact)) / np.float32(math.log(REL_MAX_DIST / max_exact))
                         * np.float32(REL_BUCKETS - max_exact)).astype(np.int32)
    large = np.minimum(large, REL_BUCKETS - 1)
    return np.where(dist < max_exact, dist, large)


def _attn_bias(rel_bias):
    blk = ATT_BLOCK
    period = 3 * blk + 1
    k = np.arange(-(blk - 1), 2 * blk)
    tabs = []
    for window, dilation in ATT_PATTERNS:
        delta = blk - k
        valid = (delta >= 0) & (delta <= window // dilation)
        bucket = _t5_bucket_np(np.maximum(delta, 0) * dilation)
        vals = jnp.where(valid[:, None], rel_bias.astype(F32)[bucket] * LOG2E, NEG_BIG)
        v = jnp.full((period, ATT_HEADS), NEG_BIG, F32).at[k % period].set(vals).T
        tiled = jnp.tile(v, (1, blk))[:, :blk * (period - 1)]
        tabs.append(tiled.reshape(ATT_HEADS, blk, period - 1)[:, :, :2 * blk])
    return jnp.concatenate(tabs, axis=0)


def _attn_kernel(q_ref, k_ref, v_ref, qg_ref, kg_ref, bias_ref, o_ref,
                 qn_ref, kn_ref, vn_ref, q16_ref, k16_ref, v16_ref,
                 on_ref, lse_ref, on16_ref, lse16_ref, btab_ref, sc0_ref, sc1_ref):
    L = q_ref.shape[0]
    blk = ATT_BLOCK
    n_pat = len(ATT_PATTERNS)
    max_dil = ATT_PATTERNS[-1][1]
    tail = ATT_TAIL
    p16 = ATT_P16
    step = pl.program_id(1)
    first = step == 0
    cur = step % 2
    half_lane = lax.broadcasted_iota(jnp.int32, (blk, LANES), 1) < ATT_DH
    mean_blk = jnp.where(
        lax.broadcasted_iota(jnp.int32, (LANES, LANES), 0) // ATT_DH
        == lax.broadcasted_iota(jnp.int32, (LANES, LANES), 1) // ATT_DH, 1.0 / ATT_DH, 0.0).astype(BF16)

    @pl.when(first)
    def _():
        kn_ref[:, 0:tail, :] = jnp.zeros((2, tail, LANES), F32)
        vn_ref[:, 0:tail, :] = jnp.zeros((2, tail, LANES), F32)
        k16_ref[...] = jnp.zeros_like(k16_ref)
        v16_ref[...] = jnp.zeros_like(v16_ref)
        prev_cols = lax.broadcasted_iota(jnp.int32, (blk, 2 * blk), 1) < blk
        for t in range(n_pat * ATT_HEADS):
            btab_ref[0, t] = bias_ref[t]
            btab_ref[1, t] = jnp.where(prev_cols, NEG_BIG, bias_ref[t])

    @pl.when(step == 1)
    def _():
        for t in range(n_pat * ATT_HEADS):
            btab_ref[1, t] = bias_ref[t]

    @pl.when(jnp.logical_not(first))
    def _():
        kn_ref[:, 0:tail, :] = kn_ref[:, L:L + tail, :]
        vn_ref[:, 0:tail, :] = vn_ref[:, L:L + tail, :]

    def head_norm(x, g):
        ms = _dot((x * x).astype(BF16), mean_blk)
        return x * (lax.rsqrt(ms + EPS) * g)

    q_gain = qg_ref[...] * (ATT_DH ** -0.5 * LOG2E)
    for p in range(2):
        ls = slice(p * LANES, (p + 1) * LANES)
        qn_ref[p] = head_norm(q_ref[:, ls].astype(F32), q_gain)
        kn_ref[p, tail:tail + L, :] = head_norm(k_ref[:, ls].astype(F32), kg_ref[...])
        vn_ref[p, tail:tail + L, :] = v_ref[:, ls].astype(F32)

    def regroup(n, carry):
        src = pl.multiple_of(n * max_dil, max_dil)
        dst = pl.ds(n, max_dil, stride=p16)
        for p in range(2):
            q16_ref[p, dst, :] = qn_ref[p, pl.ds(src, max_dil), :]
            k16_ref[p, cur, dst, :] = kn_ref[p, pl.ds(tail + src, max_dil), :]
            v16_ref[p, cur, dst, :] = vn_ref[p, pl.ds(tail + src, max_dil), :]
        return carry

    lax.fori_loop(0, L // max_dil, regroup, 0, unroll=4)

    def rows(start, n, dil):
        return pl.ds(start, n) if dil == 1 else pl.ds(start, n, stride=dil)

    def q_rows(ci, dil):
        return (ci % dil) + dil * blk * (ci // dil)

    def slab_rows(ci):
        r0 = ci * p16
        return pl.ds(r0 if isinstance(r0, int) else pl.multiple_of(r0, SUBLANES), blk)

    def load_q(pi, p, ci):
        dil = ATT_PATTERNS[pi][1]
        if dil == max_dil:
            return q16_ref[p, slab_rows(ci), :]
        return qn_ref[p, rows(q_rows(ci, dil), blk, dil), :]

    def load_kv(nat_ref, slab_ref, pi, p, ci):
        dil = ATT_PATTERNS[pi][1]
        if dil == max_dil:
            return jnp.concatenate([slab_ref[p, 1 - cur, slab_rows(ci), :], slab_ref[p, cur, slab_rows(ci), :]],
                                   axis=0)
        return nat_ref[p, rows(tail + q_rows(ci, dil) - dil * blk, 2 * blk, dil), :]

    def store_out(pi, p, ci, on, lse):
        dil = ATT_PATTERNS[pi][1]
        if dil == max_dil:
            on16_ref[p, slab_rows(ci), :] = on
            lse16_ref[p, slab_rows(ci), :] = lse
        else:
            on_ref[pi, p, rows(q_rows(ci, dil), blk, dil), :] = on
            lse_ref[pi, p, rows(q_rows(ci, dil), blk, dil), :] = lse

    ones_cols = jnp.ones((2 * blk, LANES), BF16)
    n_combo = L // blk

    def scores(pi, ci, s_ref):
        tsel = jnp.where(ci < ATT_PATTERNS[pi][1], 1, 0)
        for p in range(2):
            qb = load_q(pi, p, ci)
            kb = load_kv(kn_ref, k16_ref, pi, p, ci).astype(BF16)
            for hh in range(2):
                sel = half_lane if hh == 0 else jnp.logical_not(half_lane)
                h = 2 * p + hh
                s_ref[h] = (_dot_nt(jnp.where(sel, qb, 0.0).astype(BF16), kb)
                            + btab_ref[tsel, pi * ATT_HEADS + h])

    def softmax_pv(pi, ci, s_ref):
        for p in range(2):
            vb = jnp.concatenate([load_kv(vn_ref, v16_ref, pi, p, ci).astype(BF16), ones_cols], axis=1)
            mxs, ovs = [], []
            for hh in range(2):
                sc = s_ref[2 * p + hh]
                mx = jnp.max(sc, axis=-1, keepdims=True)
                ovs.append(_dot(jnp.exp2(sc - mx).astype(BF16), vb))
                mxs.append(mx)
            den = jnp.where(half_lane, ovs[0][:, LANES:], ovs[1][:, LANES:])
            num = jnp.where(half_lane, ovs[0][:, :LANES], ovs[1][:, :LANES])
            store_out(pi, p, ci, num * (1.0 / den), jnp.where(half_lane, mxs[0], mxs[1]) + jnp.log2(den))

    sc_refs = (sc0_ref, sc1_ref)
    unroll = ATT_UNROLL
    scores(0, 0, sc0_ref)
    for pi in range(n_pat):
        def group(c0, last, pi=pi):
            for k in range(unroll):
                if not last or k + 1 < unroll:
                    scores(pi, c0 + k + 1, sc_refs[(k + 1) % 2])
                elif pi + 1 < n_pat:
                    scores(pi + 1, 0, sc_refs[(k + 1) % 2])
                softmax_pv(pi, c0 + k, sc_refs[k % 2])

        if n_combo > unroll:
            def body(j, carry, group=group):
                group(j * unroll, False)
                return carry

            lax.fori_loop(0, n_combo // unroll - 1, body, 0)
        group(n_combo - unroll, True)

    def merge(ti, carry):
        t0 = pl.multiple_of(ti * blk, blk)
        g0 = ti * (blk // max_dil)

        def regrouped(ref, p):
            return jnp.concatenate([ref[p, pl.ds(g0 + g, max_dil, stride=p16), :]
                                    for g in range(blk // max_dil)], axis=0)

        for p in range(2):
            ls_ = [lse_ref[pi, p, pl.ds(t0, blk), :] for pi in range(n_pat - 1)] + [regrouped(lse16_ref, p)]
            os_ = [on_ref[pi, p, pl.ds(t0, blk), :] for pi in range(n_pat - 1)] + [regrouped(on16_ref, p)]
            mx = functools.reduce(jnp.maximum, ls_)
            ws = [jnp.exp2(x - mx) for x in ls_]
            num = sum(w * o for w, o in zip(ws, os_))
            o_ref[pl.ds(t0, blk), p * LANES:(p + 1) * LANES] = num / sum(ws)
        return carry

    lax.fori_loop(0, L // blk, merge, 0)


def _attn(z, q_g, k_g, bias):
    bsz, s, _ = z.shape
    L = ATT_L
    assert s % L == 0
    gq = jnp.concatenate([q_g, q_g]).reshape(1, LANES).astype(F32)
    gk = jnp.concatenate([k_g, k_g]).reshape(1, LANES).astype(F32)
    n_pat = len(ATT_PATTERNS)
    slabs = ATT_PATTERNS[-1][1] * ATT_P16

    def zspec(j):
        return pl.BlockSpec((None, L, D_GROUP), lambda b, i, j=j: (b, i, j))

    return pl.pallas_call(
        _attn_kernel,
        grid=(bsz, s // L),
        in_specs=[zspec(7), zspec(8), zspec(9),
                  pl.BlockSpec((1, LANES), lambda b, i: (0, 0)),
                  pl.BlockSpec((1, LANES), lambda b, i: (0, 0)),
                  pl.BlockSpec(bias.shape, lambda b, i: (0, 0, 0))],
        out_specs=pl.BlockSpec((None, L, D_GROUP), lambda b, i: (b, i, 0)),
        out_shape=jax.ShapeDtypeStruct((bsz, s, D_GROUP), F32),
        scratch_shapes=[pltpu.VMEM((2, L, LANES), F32),
                        pltpu.VMEM((2, ATT_TAIL + L, LANES), F32),
                        pltpu.VMEM((2, ATT_TAIL + L, LANES), F32),
                        pltpu.VMEM((2, slabs, LANES), F32),
                        pltpu.VMEM((2, 2, slabs, LANES), F32),
                        pltpu.VMEM((2, 2, slabs, LANES), F32),
                        pltpu.VMEM((n_pat - 1, 2, L, LANES), F32),
                        pltpu.VMEM((n_pat - 1, 2, L, LANES), F32),
                        pltpu.VMEM((2, slabs, LANES), F32),
                        pltpu.VMEM((2, slabs, LANES), F32),
                        pltpu.VMEM((2,) + bias.shape, F32),
                        pltpu.VMEM((ATT_HEADS, ATT_BLOCK, 2 * ATT_BLOCK), F32),
                        pltpu.VMEM((ATT_HEADS, ATT_BLOCK, 2 * ATT_BLOCK), F32)],
        compiler_params=_cparams(("parallel", "arbitrary")),
        name="dilated_attn",
    )(z, z, z, gq, gk, bias)


def _post_kernel(x_ref, ya_ref, yc_ref, yd_ref, val0_ref, gate0_ref, valn_ref, gaten_ref,
                 cw_ref, cb_ref, clg_ref, clb_ref, gmix_ref, wout_ref, gmlp_ref, wup_ref, wdn_ref,
                 o_ref, h_ref, yb_ref):
    conv_params = (cw_ref, cb_ref, clg_ref, clb_ref)

    @pl.when(pl.program_id(1) == 0)
    def _():
        for step in _conv_tile_steps(val0_ref, gate0_ref, *conv_params, yb_ref, h_ref, first=True):
            step()

    conv_steps = _conv_tile_steps(valn_ref, gaten_ref, *conv_params, yb_ref, h_ref, first=False)
    ys = (ya_ref[...], yb_ref[...], yc_ref[...], yd_ref[...])
    conv_steps.pop(0)()
    acc = x_ref[...]
    for j, y in enumerate(ys):
        cs = slice(j * D_GROUP, (j + 1) * D_GROUP)
        yn = y * lax.rsqrt(jnp.mean(y * y, axis=-1, keepdims=True) + EPS) * gmix_ref[:, cs]
        acc = acc + _dot(yn.astype(BF16), wout_ref[cs, :])
    h = (acc * lax.rsqrt(jnp.mean(acc * acc, axis=-1, keepdims=True) + EPS) * gmlp_ref[...]).astype(BF16)
    mlp = jnp.zeros_like(acc)
    ff_tile = D_MODEL
    per_dot = len(conv_steps) * ff_tile // (2 * D_FF)
    for f0 in range(0, D_FF, ff_tile):
        hm = jnp.maximum(_dot(h, wup_ref[:, f0:f0 + ff_tile]), 0.0)
        for _ in range(per_dot):
            conv_steps.pop(0)()
        mlp = mlp + _dot((hm * hm).astype(BF16), wdn_ref[f0:f0 + ff_tile, :])
        for _ in range(per_dot):
            conv_steps.pop(0)()
    assert not conv_steps
    o_ref[...] = acc + mlp


def _post(x, z, y_a, y_c, y_d, conv_w, conv_b, ln_g, ln_b, gmix, gmlp, wout_all, wup_all, wdn_all, layer):
    bsz, s, d = x.shape
    n_tiles = s // ROW_TILE
    row = lambda n: pl.BlockSpec((None, ROW_TILE, n), lambda b, i: (b, i, 0))
    const = lambda shape: pl.BlockSpec(shape, lambda b, i: (0, 0), pipeline_mode=pl.Buffered(1))
    weight = lambda shape: pl.BlockSpec((None,) + shape, lambda b, i: (layer, 0, 0),
                                        pipeline_mode=pl.Buffered(1))
    z_first = lambda j: pl.BlockSpec((None, ROW_TILE, D_GROUP), lambda b, i: (b, 0, j))
    z_next = lambda j: pl.BlockSpec((None, ROW_TILE, D_GROUP),
                                    lambda b, i: (b, jnp.minimum(i + 1, n_tiles - 1), j))
    vec = lambda v: v.reshape(1, -1)
    return pl.pallas_call(
        _post_kernel,
        grid=(bsz, n_tiles),
        in_specs=[row(d), row(D_GROUP), row(D_GROUP), row(D_GROUP),
                  z_first(4), z_first(5), z_next(4), z_next(5),
                  const((CONV_WIDTH, D_GROUP)), const((1, D_GROUP)), const((1, D_GROUP)), const((1, D_GROUP)),
                  const((1, d)), weight((d, d)), const((1, d)), weight((d, D_FF)), weight((D_FF, d))],
        out_specs=row(d),
        out_shape=jax.ShapeDtypeStruct((bsz, s, d), F32),
        scratch_shapes=[pltpu.VMEM((CONV_HALO + ROW_TILE, D_GROUP), F32),
                        pltpu.VMEM((ROW_TILE, D_GROUP), F32)],
        compiler_params=_cparams(("parallel", "arbitrary")),
        name="outproj_mlp",
    )(x, y_a, y_c, y_d, z, z, z, z, conv_w, vec(conv_b), vec(ln_g), vec(ln_b),
      vec(gmix), wout_all, vec(gmlp), wup_all, wdn_all)


def kernel(x, norm_mix_g, w_in, hgrn_lb_logits, hgrn_norm_g, conv_w, conv_b, conv_ln_g, conv_ln_b, s5_lambda_re, s5_lambda_im, s5_b_re, s5_b_im, s5_c_re, s5_c_im, s5_d, s5_log_dt, s5_w_glu, attn_q_norm_g, attn_k_norm_g, rel_bias, mix_out_norm_g, w_out, norm_mlp_g, w_mlp_up, w_mlp_down):
    depth = w_in.shape[0]
    lb_sm = jax.nn.softmax(hgrn_lb_logits.astype(F32), axis=0)
    lb_all = jnp.maximum(jnp.cumsum(lb_sm, axis=0) - lb_sm[0], 0.0)
    bias = _attn_bias(rel_bias)
    w_in, w_out, w_mlp_up, w_mlp_down = (w.astype(BF16) for w in (w_in, w_out, w_mlp_up, w_mlp_down))
    s5_a, s5_bm, s5_cm = jax.vmap(_s5_matrices)(s5_lambda_re, s5_lambda_im, s5_b_re, s5_b_im,
                                                 s5_c_re, s5_c_im, s5_log_dt)
    s5_w_glu = s5_w_glu.astype(BF16)
    for l in range(depth):
        z = _inproj(x, norm_mix_g[l], w_in, l)
        y_a = _hgrn(z, lb_all[l], hgrn_norm_g[l])
        y_c = _s5(z, s5_a, s5_bm, s5_cm, s5_d[l], s5_w_glu, l)
        y_d = _attn(z, attn_q_norm_g[l], attn_k_norm_g[l], bias)
        x = _post(x, z, y_a, y_c, y_d, conv_w[l], conv_b[l], conv_ln_g[l], conv_ln_b[l],
                  mix_out_norm_g[l], norm_mlp_g[l], w_out, w_mlp_up, w_mlp_down, l)
    return x
```

```python
import functools
import math

import numpy as np
import jax
import jax.numpy as jnp
from jax import lax
from jax.experimental import pallas as pl
from jax.experimental.pallas import tpu as pltpu

F32 = jnp.float32
BF16 = jnp.bfloat16

D_MODEL = 1024
D_GROUP = 256
N_MIXERS = 4
HG_HEADS = 4
HG_DK = 64
LB_FLOOR = 1e-30
CONV_WIDTH = 31
S5_CH = 16
S5_GROUPS = 16
S5_STATE = 64
ATT_HEADS = 4
ATT_DH = 64
ATT_PATTERNS = ((128, 1), (512, 4), (2048, 16))
ATT_BLOCK = 128
REL_BUCKETS = 32
REL_MAX_DIST = 2048
NEG_BIG = -1e30
D_FF = 4 * D_MODEL
N_IN_SLICES = 10
D_IN = N_IN_SLICES * D_GROUP
EPS = 1e-6
LOG2E = 1.4426950408889634

LANES = 128
SUBLANES = 8
VMEM_LIMIT = 56 * 1024 * 1024

ROW_TILE = 512
INPROJ_TILE = 1024
HG_CHUNK = 128
HG_BLOCK = 1024
CONV_SUB = 64
CONV_HALO = 32
S5_BLOCK = 512
S5_PITCH = S5_BLOCK + SUBLANES // 2
S5_TILES = 2 * S5_GROUPS * S5_STATE // LANES
ATT_L = 2048
ATT_TAIL = 512
ATT_P16 = ATT_BLOCK + SUBLANES
ATT_UNROLL = 16


def _cparams(sem):
    return pltpu.CompilerParams(dimension_semantics=sem, vmem_limit_bytes=VMEM_LIMIT)


def _sigmoid(x):
    return 1.0 / (1.0 + jnp.exp(-x))


def _dot(a, b):
    return jnp.dot(a, b, preferred_element_type=F32)


def _dot_nt(a, b):
    return lax.dot_general(a, b, (((1,), (1,)), ((), ())), preferred_element_type=F32)


def _dot_tn(a, b):
    return lax.dot_general(a, b, (((0,), (0,)), ((), ())), preferred_element_type=F32)


def _inproj_kernel(x_ref, g_ref, w_ref, z_ref):
    x = x_ref[...]
    h = x * lax.rsqrt(jnp.mean(x * x, axis=-1, keepdims=True) + EPS) * g_ref[...]
    z_ref[...] = _dot(h.astype(BF16), w_ref[...]).astype(z_ref.dtype)


def _inproj(x, g, w_all, layer):
    bsz, s, d = x.shape
    n = w_all.shape[2]
    return pl.pallas_call(
        _inproj_kernel,
        grid=(bsz, s // INPROJ_TILE),
        in_specs=[
            pl.BlockSpec((None, INPROJ_TILE, d), lambda b, i: (b, i, 0)),
            pl.BlockSpec((1, d), lambda b, i: (0, 0)),
            pl.BlockSpec((None, d, n), lambda b, i: (layer, 0, 0)),
        ],
        out_specs=pl.BlockSpec((None, INPROJ_TILE, n), lambda b, i: (b, i, 0)),
        out_shape=jax.ShapeDtypeStruct((bsz, s, n), BF16),
        compiler_params=_cparams(("parallel", "parallel")),
        name="inproj",
    )(x, g.reshape(1, d), w_all)


def _hold(gc, m):
    c, w = gc.shape
    if 2 * m >= SUBLANES:
        x = gc.reshape(c // (2 * m), 2 * m, w)
        return jnp.broadcast_to(x[:, m - 1:m, :], x.shape).reshape(c, w)
    x = gc.reshape(c // SUBLANES, SUBLANES, w)
    rows = lax.broadcasted_iota(jnp.int32, x.shape, 1)
    out = None
    for p in range(SUBLANES // (2 * m)):
        e = p * 2 * m + m - 1
        b = jnp.broadcast_to(x[:, e:e + 1, :], x.shape)
        out = b if out is None else jnp.where(rows >= p * 2 * m, b, out)
    return out.reshape(c, w)


def _hgrn_kernel(q_ref, f_ref, i_ref, g_ref, lb_ref, ng_ref, lvl_ref, o_ref, st_ref):
    @pl.when(pl.program_id(1) == 0)
    def _():
        st_ref[...] = jnp.zeros_like(st_ref)

    c = HG_CHUNK
    w = D_GROUP
    n_levels = int(math.log2(c))
    lane_head = lax.broadcasted_iota(jnp.int32, (c, w), 1) // HG_DK
    row = lax.broadcasted_iota(jnp.int32, (c, w), 0)
    blk = (lax.broadcasted_iota(jnp.int32, (w, w), 0) // HG_DK
           == lax.broadcasted_iota(jnp.int32, (w, w), 1) // HG_DK)
    ones_blk = jnp.where(blk, 1.0, 0.0).astype(BF16)
    lb = jnp.clip(lb_ref[...], 0.0, 1.0 - 1e-6)
    lb_floor = jnp.maximum(lb, LB_FLOOR)
    tri = jnp.where(lax.broadcasted_iota(jnp.int32, (c, c), 0) >= lax.broadcasted_iota(jnp.int32, (c, c), 1),
                    1.0, 0.0).astype(BF16)
    sgn = [jnp.where((row & (2 ** lv)) != 0, 1.0, -1.0) for lv in range(n_levels)]
    grp = SUBLANES

    def stack_heads(t):
        return jnp.concatenate(
            [jnp.where(lane_head == h, t, 0.0).astype(BF16) for h in range(HG_HEADS)], axis=0)

    def chunk(r0, st):
        q = q_ref[r0:r0 + c, :].astype(F32)
        fz = f_ref[r0:r0 + c, :].astype(F32)
        v = i_ref[r0:r0 + c, :].astype(F32)
        gate = g_ref[r0:r0 + c, :].astype(F32)
        gated = (1.0 - lb) * _sigmoid(fz)
        lf = jnp.log2(lb_floor + gated)
        kk = (1.0 - lb) - gated
        hi = lf.astype(BF16)
        r1 = lf - hi.astype(F32)
        mid = r1.astype(BF16)
        lo = (r1 - mid.astype(F32)).astype(BF16)
        gc = _dot(tri, hi) + _dot(tri, mid) + _dot(tri, lo)

        q_b = q.astype(BF16)
        k_b = [jnp.where(lane_head == h, kk, 0.0).astype(BF16) for h in range(HG_HEADS)]
        n_grp = c // grp
        scores = [None] * n_grp
        for lv in range(n_levels):
            m = 2 ** lv
            e = jnp.exp2((gc - _hold(gc, m)) * sgn[lv]).astype(BF16)
            p_lv = _dot_nt(q_b * e, jnp.concatenate([kb * e for kb in k_b], axis=0))
            for gi in range(n_grp):
                if m >= grp and (gi * grp) & m == 0:
                    continue
                rs = slice(gi * grp, (gi + 1) * grp)
                old = 0.0 if scores[gi] is None else scores[gi]
                scores[gi] = jnp.where(lvl_ref[rs, :] == lv, p_lv[rs], old)
        o = _dot(jnp.concatenate(scores, axis=0).astype(BF16), stack_heads(v))
        qk = q * kk
        qk_hi = qk.astype(BF16)
        qk_lo = (qk - qk_hi.astype(F32)).astype(BF16)
        o = o + (_dot(qk_hi, ones_blk) + _dot(qk_lo, ones_blk)) * v
        o = o + _dot_nt((q * jnp.exp2(gc)).astype(BF16), st.astype(BF16))
        g_last = gc[c - 1:c, :]
        upd = _dot_tn(v.astype(BF16), (kk * jnp.exp2(g_last - gc)).astype(BF16))
        st = st * jnp.exp2(g_last) + jnp.where(blk, upd, 0.0)

        ms = _dot((o * o).astype(BF16), ones_blk) * (1.0 / HG_DK)
        y = o * lax.rsqrt(ms + EPS) * ng_ref[...]
        o_ref[r0:r0 + c, :] = y * (gate * _sigmoid(gate))
        return st

    st = st_ref[...]
    for r0 in range(0, q_ref.shape[0], c):
        st = chunk(r0, st)
    st_ref[...] = st


def _hgrn_levels():
    t = np.arange(HG_CHUNK)[:, None]
    s = np.arange(HG_CHUNK)[None, :]
    x = np.maximum(t ^ s, 1)
    lv = np.where(s < t, np.floor(np.log2(x)).astype(np.int32), -1).astype(np.int32)
    return np.tile(lv, (1, HG_HEADS))


def _hgrn(z, lb, norm_g):
    bsz, s, _ = z.shape
    tb = min(HG_BLOCK, s)

    def zspec(j):
        return pl.BlockSpec((None, tb, D_GROUP), lambda b, i, j=j: (b, i, j))

    const = lambda shape: pl.BlockSpec(shape, lambda b, i: (0, 0))
    return pl.pallas_call(
        _hgrn_kernel,
        grid=(bsz, s // tb),
        in_specs=[zspec(0), zspec(1), zspec(2), zspec(3),
                  const((1, D_GROUP)), const((1, D_GROUP)), const((HG_CHUNK, HG_HEADS * HG_CHUNK))],
        out_specs=pl.BlockSpec((None, tb, D_GROUP), lambda b, i: (b, i, 0)),
        out_shape=jax.ShapeDtypeStruct((bsz, s, D_GROUP), F32),
        scratch_shapes=[pltpu.VMEM((D_GROUP, D_GROUP), F32)],
        compiler_params=_cparams(("parallel", "arbitrary")),
        name="hgrn2",
    )(z, z, z, z, lb.reshape(1, D_GROUP), norm_g.reshape(1, D_GROUP), jnp.asarray(_hgrn_levels()))


def _conv_tile_steps(val_ref, gate_ref, w_ref, b_ref, lg_ref, lbias_ref, o_ref, h_ref, first):
    tb = val_ref.shape[0]
    base = CONV_HALO - (CONV_WIDTH - 1)

    def glu():
        if first:
            h_ref[0:CONV_HALO, :] = jnp.zeros((CONV_HALO, D_GROUP), F32)
        else:
            h_ref[0:CONV_HALO, :] = h_ref[tb:tb + CONV_HALO, :]
        h_ref[CONV_HALO:CONV_HALO + tb, :] = val_ref[...].astype(F32) * _sigmoid(gate_ref[...].astype(F32))

    def sub_tile(t0):
        acc = jnp.broadcast_to(b_ref[...], (CONV_SUB, D_GROUP))
        for c in range(SUBLANES):
            part = None
            rows = CONV_SUB + (SUBLANES if c else 0)
            for j in range(CONV_WIDTH):
                if (base + j) % SUBLANES != c:
                    continue
                r0 = t0 + base + j - c
                term = w_ref[j:j + 1, :] * h_ref[r0:r0 + rows, :]
                part = term if part is None else part + term
            acc = acc + part[c:c + CONV_SUB, :]
        mu = jnp.mean(acc, axis=-1, keepdims=True)
        d = acc - mu
        var = jnp.mean(d * d, axis=-1, keepdims=True)
        y = d * lax.rsqrt(var + EPS) * lg_ref[...] + lbias_ref[...]
        o_ref[t0:t0 + CONV_SUB, :] = y * _sigmoid(y)

    return [glu] + [functools.partial(sub_tile, t0) for t0 in range(0, tb, CONV_SUB)]


def _s5_kernel(u_ref, a_ref, bm_ref, cm_ref, d_ref, wg_ref, o_ref, bu_ref, hs_ref, st_ref):
    nb, t_len, _ = u_ref.shape
    half = S5_TILES // 2

    @pl.when(pl.program_id(0) == 0)
    def _():
        st_ref[...] = jnp.zeros_like(st_ref)

    ar = a_ref[0:half, :]
    ai = a_ref[half:S5_TILES, :]

    for b in range(nb):
        bu = _dot(u_ref[b], bm_ref[...])
        for i in range(S5_TILES):
            bu_ref[b, i * S5_PITCH:i * S5_PITCH + t_len, :] = bu[:, i * LANES:(i + 1) * LANES]

    hs = [(st_ref[b, 0:half, :], st_ref[b, half:S5_TILES, :]) for b in range(nb)]
    for t in range(t_len):
        for b in range(nb):
            hr, hi = hs[b]
            br = bu_ref[b, pl.ds(t, half, stride=S5_PITCH), :]
            bi = bu_ref[b, pl.ds(half * S5_PITCH + t, half, stride=S5_PITCH), :]
            nr = ar * hr - ai * hi + br
            ni = ar * hi + ai * hr + bi
            hs_ref[b, pl.ds(t, half, stride=S5_PITCH), :] = nr
            hs_ref[b, pl.ds(half * S5_PITCH + t, half, stride=S5_PITCH), :] = ni
            hs[b] = (nr, ni)
    for b in range(nb):
        st_ref[b, 0:half, :] = hs[b][0]
        st_ref[b, half:S5_TILES, :] = hs[b][1]

    for b in range(nb):
        u = u_ref[b].astype(F32)
        hcat = jnp.concatenate(
            [hs_ref[b, i * S5_PITCH:i * S5_PITCH + t_len, :] for i in range(S5_TILES)], axis=1)
        y = _dot(hcat.astype(BF16), cm_ref[...]) + d_ref[...] * u
        y = 0.5 * y * (1.0 + jnp.tanh(math.sqrt(2.0 / math.pi) * (y + 0.044715 * (y * y * y))))
        o_ref[b] = y * _sigmoid(_dot(y.astype(BF16), wg_ref[...]))


def _s5_matrices(lam_re, lam_im, b_re, b_im, c_re, c_im, log_dt):
    dt = jnp.exp(log_dt.astype(F32))[:, None]
    lr = lam_re.astype(F32)
    li = lam_im.astype(F32)
    mag = jnp.exp(lr * dt)
    a_re = mag * jnp.cos(li * dt)
    a_im = mag * jnp.sin(li * dt)
    den = lr * lr + li * li
    coef_re = ((a_re - 1.0) * lr + a_im * li) / den
    coef_im = (a_im * lr - (a_re - 1.0) * li) / den
    br = b_re.astype(F32)
    bi = b_im.astype(F32)
    bbar_re = coef_re[..., None] * br - coef_im[..., None] * bi
    bbar_im = coef_re[..., None] * bi + coef_im[..., None] * br
    eye = jnp.eye(S5_GROUPS, dtype=F32)
    n_state = S5_GROUPS * S5_STATE

    def b_block(bb):
        return jnp.einsum('gpc,gh->gchp', bb, eye).reshape(D_GROUP, n_state)

    def c_block(cc):
        return jnp.einsum('gcp,gh->gphc', cc, eye).reshape(n_state, D_GROUP)

    bm = jnp.concatenate([b_block(bbar_re), b_block(bbar_im)], axis=1).astype(BF16)
    cm = jnp.concatenate([c_block(c_re.astype(F32)), -c_block(c_im.astype(F32))], axis=0).astype(BF16)
    a = jnp.concatenate([a_re.reshape(n_state // LANES, LANES), a_im.reshape(n_state // LANES, LANES)], 0)
    return a, bm, cm


def _s5(z, a_all, bm_all, cm_all, d_skip, wglu_all, layer):
    bsz, s, _ = z.shape
    tb = min(S5_BLOCK, s)
    assert tb == S5_BLOCK
    const = lambda shape: pl.BlockSpec(shape, lambda i: (0, 0))
    per_layer = lambda shape: pl.BlockSpec((None,) + shape, lambda i: (layer, 0, 0))
    n_state2 = 2 * S5_GROUPS * S5_STATE
    return pl.pallas_call(
        _s5_kernel,
        grid=(s // tb,),
        in_specs=[pl.BlockSpec((bsz, tb, D_GROUP), lambda i: (0, i, 6)),
                  per_layer((S5_TILES, LANES)), per_layer((D_GROUP, n_state2)), per_layer((n_state2, D_GROUP)),
                  const((1, D_GROUP)), per_layer((D_GROUP, D_GROUP))],
        out_specs=pl.BlockSpec((bsz, tb, D_GROUP), lambda i: (0, i, 0)),
        out_shape=jax.ShapeDtypeStruct((bsz, s, D_GROUP), F32),
        scratch_shapes=[pltpu.VMEM((bsz, S5_TILES * S5_PITCH, LANES), F32),
                        pltpu.VMEM((bsz, S5_TILES * S5_PITCH, LANES), F32),
                        pltpu.VMEM((bsz, S5_TILES, LANES), F32)],
        compiler_params=_cparams(("arbitrary",)),
        name="s5",
    )(z, a_all, bm_all, cm_all, d_skip.reshape(1, D_GROUP), wglu_all)


def _t5_bucket_np(dist):
    max_exact = REL_BUCKETS // 2
    distf = np.maximum(dist, max_exact).astype(np.float32)
    large = max_exact + (np.log(distf / np.float32(max_exact)) / np.float32(math.log(REL_MAX_DIST / max_exact))
                         * np.float32(REL_BUCKETS - max_exact)).astype(np.int32)
    large = np.minimum(large, REL_BUCKETS - 1)
    return np.where(dist < max_exact, dist, large)


def _attn_bias(rel_bias):
    blk = ATT_BLOCK
    period = 3 * blk + 1
    k = np.arange(-(blk - 1), 2 * blk)
    tabs = []
    for window, dilation in ATT_PATTERNS:
        delta = blk - k
        valid = (delta >= 0) & (delta <= window // dilation)
        bucket = _t5_bucket_np(np.maximum(delta, 0) * dilation)
        vals = jnp.where(valid[:, None], rel_bias.astype(F32)[bucket] * LOG2E, NEG_BIG)
        v = jnp.full((period, ATT_HEADS), NEG_BIG, F32).at[k % period].set(vals).T
        tiled = jnp.tile(v, (1, blk))[:, :blk * (period - 1)]
        tabs.append(tiled.reshape(ATT_HEADS, blk, period - 1)[:, :, :2 * blk])
    return jnp.concatenate(tabs, axis=0)


def _attn_kernel(q_ref, k_ref, v_ref, qg_ref, kg_ref, bias_ref, o_ref,
                 qn_ref, kn_ref, vn_ref, q16_ref, k16_ref, v16_ref,
                 on_ref, lse_ref, on16_ref, lse16_ref, btab_ref, sc0_ref, sc1_ref):
    L = q_ref.shape[0]
    blk = ATT_BLOCK
    n_pat = len(ATT_PATTERNS)
    max_dil = ATT_PATTERNS[-1][1]
    tail = ATT_TAIL
    p16 = ATT_P16
    step = pl.program_id(1)
    first = step == 0
    cur = step % 2
    half_lane = lax.broadcasted_iota(jnp.int32, (blk, LANES), 1) < ATT_DH
    mean_blk = jnp.where(
        lax.broadcasted_iota(jnp.int32, (LANES, LANES), 0) // ATT_DH
        == lax.broadcasted_iota(jnp.int32, (LANES, LANES), 1) // ATT_DH, 1.0 / ATT_DH, 0.0).astype(BF16)

    @pl.when(first)
    def _():
        kn_ref[:, 0:tail, :] = jnp.zeros((2, tail, LANES), F32)
        vn_ref[:, 0:tail, :] = jnp.zeros((2, tail, LANES), F32)
        k16_ref[...] = jnp.zeros_like(k16_ref)
        v16_ref[...] = jnp.zeros_like(v16_ref)
        prev_cols = lax.broadcasted_iota(jnp.int32, (blk, 2 * blk), 1) < blk
        for t in range(n_pat * ATT_HEADS):
            btab_ref[0, t] = bias_ref[t]
            btab_ref[1, t] = jnp.where(prev_cols, NEG_BIG, bias_ref[t])

    @pl.when(step == 1)
    def _():
        for t in range(n_pat * ATT_HEADS):
            btab_ref[1, t] = bias_ref[t]

    @pl.when(jnp.logical_not(first))
    def _():
        kn_ref[:, 0:tail, :] = kn_ref[:, L:L + tail, :]
        vn_ref[:, 0:tail, :] = vn_ref[:, L:L + tail, :]

    def head_norm(x, g):
        ms = _dot((x * x).astype(BF16), mean_blk)
        return x * (lax.rsqrt(ms + EPS) * g)

    q_gain = qg_ref[...] * (ATT_DH ** -0.5 * LOG2E)
    for p in range(2):
        ls = slice(p * LANES, (p + 1) * LANES)
        qn_ref[p] = head_norm(q_ref[:, ls].astype(F32), q_gain)
        kn_ref[p, tail:tail + L, :] = head_norm(k_ref[:, ls].astype(F32), kg_ref[...])
        vn_ref[p, tail:tail + L, :] = v_ref[:, ls].astype(F32)

    def regroup(n, carry):
        src = pl.multiple_of(n * max_dil, max_dil)
        dst = pl.ds(n, max_dil, stride=p16)
        for p in range(2):
            q16_ref[p, dst, :] = qn_ref[p, pl.ds(src, max_dil), :]
            k16_ref[p, cur, dst, :] = kn_ref[p, pl.ds(tail + src, max_dil), :]
            v16_ref[p, cur, dst, :] = vn_ref[p, pl.ds(tail + src, max_dil), :]
        return carry

    lax.fori_loop(0, L // max_dil, regroup, 0, unroll=4)

    def rows(start, n, dil):
        return pl.ds(start, n) if dil == 1 else pl.ds(start, n, stride=dil)

    def q_rows(ci, dil):
        return (ci % dil) + dil * blk * (ci // dil)

    def slab_rows(ci):
        r0 = ci * p16
        return pl.ds(r0 if isinstance(r0, int) else pl.multiple_of(r0, SUBLANES), blk)

    def load_q(pi, p, ci):
        dil = ATT_PATTERNS[pi][1]
        if dil == max_dil:
            return q16_ref[p, slab_rows(ci), :]
        return qn_ref[p, rows(q_rows(ci, dil), blk, dil), :]

    def load_kv(nat_ref, slab_ref, pi, p, ci):
        dil = ATT_PATTERNS[pi][1]
        if dil == max_dil:
            return jnp.concatenate([slab_ref[p, 1 - cur, slab_rows(ci), :], slab_ref[p, cur, slab_rows(ci), :]],
                                   axis=0)
        return nat_ref[p, rows(tail + q_rows(ci, dil) - dil * blk, 2 * blk, dil), :]

    def store_out(pi, p, ci, on, lse):
        dil = ATT_PATTERNS[pi][1]
        if dil == max_dil:
            on16_ref[p, slab_rows(ci), :] = on
            lse16_ref[p, slab_rows(ci), :] = lse
        else:
            on_ref[pi, p, rows(q_rows(ci, dil), blk, dil), :] = on
            lse_ref[pi, p, rows(q_rows(ci, dil), blk, dil), :] = lse

    ones_cols = jnp.ones((2 * blk, LANES), BF16)
    n_combo = L // blk

    def scores(pi, ci, s_ref):
        tsel = jnp.where(ci < ATT_PATTERNS[pi][1], 1, 0)
        for p in range(2):
            qb = load_q(pi, p, ci)
            kb = load_kv(kn_ref, k16_ref, pi, p, ci).astype(BF16)
            for hh in range(2):
                sel = half_lane if hh == 0 else jnp.logical_not(half_lane)
                h = 2 * p + hh
                s_ref[h] = (_dot_nt(jnp.where(sel, qb, 0.0).astype(BF16), kb)
                            + btab_ref[tsel, pi * ATT_HEADS + h])

    def softmax_pv(pi, ci, s_ref):
        for p in range(2):
            vb = jnp.concatenate([load_kv(vn_ref, v16_ref, pi, p, ci).astype(BF16), ones_cols], axis=1)
            mxs, ovs = [], []
            for hh in range(2):
                sc = s_ref[2 * p + hh]
                mx = jnp.max(sc, axis=-1, keepdims=True)
                ovs.append(_dot(jnp.exp2(sc - mx).astype(BF16), vb))
                mxs.append(mx)
            den = jnp.where(half_lane, ovs[0][:, LANES:], ovs[1][:, LANES:])
            num = jnp.where(half_lane, ovs[0][:, :LANES], ovs[1][:, :LANES])
            store_out(pi, p, ci, num * (1.0 / den), jnp.where(half_lane, mxs[0], mxs[1]) + jnp.log2(den))

    sc_refs = (sc0_ref, sc1_ref)
    unroll = ATT_UNROLL
    scores(0, 0, sc0_ref)
    for pi in range(n_pat):
        def group(c0, last, pi=pi):
            for k in range(unroll):
                if not last or k + 1 < unroll:
                    scores(pi, c0 + k + 1, sc_refs[(k + 1) % 2])
                elif pi + 1 < n_pat:
                    scores(pi + 1, 0, sc_refs[(k + 1) % 2])
                softmax_pv(pi, c0 + k, sc_refs[k % 2])

        if n_combo > unroll:
            def body(j, carry, group=group):
                group(j * unroll, False)
                return carry

            lax.fori_loop(0, n_combo // unroll - 1, body, 0)
        group(n_combo - unroll, True)

    def merge(ti, carry):
        t0 = pl.multiple_of(ti * blk, blk)
        g0 = ti * (blk // max_dil)

        def regrouped(ref, p):
            return jnp.concatenate([ref[p, pl.ds(g0 + g, max_dil, stride=p16), :]
                                    for g in range(blk // max_dil)], axis=0)

        for p in range(2):
            ls_ = [lse_ref[pi, p, pl.ds(t0, blk), :] for pi in range(n_pat - 1)] + [regrouped(lse16_ref, p)]
            os_ = [on_ref[pi, p, pl.ds(t0, blk), :] for pi in range(n_pat - 1)] + [regrouped(on16_ref, p)]
            mx = functools.reduce(jnp.maximum, ls_)
            ws = [jnp.exp2(x - mx) for x in ls_]
            num = sum(w * o for w, o in zip(ws, os_))
            o_ref[pl.ds(t0, blk), p * LANES:(p + 1) * LANES] = num / sum(ws)
        return carry

    lax.fori_loop(0, L // blk, merge, 0)


def _attn(z, q_g, k_g, bias):
    bsz, s, _ = z.shape
    L = ATT_L
    assert s % L == 0
    gq = jnp.concatenate([q_g, q_g]).reshape(1, LANES).astype(F32)
    gk = jnp.concatenate([k_g, k_g]).reshape(1, LANES).astype(F32)
    n_pat = len(ATT_PATTERNS)
    slabs = ATT_PATTERNS[-1][1] * ATT_P16

    def zspec(j):
        return pl.BlockSpec((None, L, D_GROUP), lambda b, i, j=j: (b, i, j))

    return pl.pallas_call(
        _attn_kernel,
        grid=(bsz, s // L),
        in_specs=[zspec(7), zspec(8), zspec(9),
                  pl.BlockSpec((1, LANES), lambda b, i: (0, 0)),
                  pl.BlockSpec((1, LANES), lambda b, i: (0, 0)),
                  pl.BlockSpec(bias.shape, lambda b, i: (0, 0, 0))],
        out_specs=pl.BlockSpec((None, L, D_GROUP), lambda b, i: (b, i, 0)),
        out_shape=jax.ShapeDtypeStruct((bsz, s, D_GROUP), F32),
        scratch_shapes=[pltpu.VMEM((2, L, LANES), F32),
                        pltpu.VMEM((2, ATT_TAIL + L, LANES), F32),
                        pltpu.VMEM((2, ATT_TAIL + L, LANES), F32),
                        pltpu.VMEM((2, slabs, LANES), F32),
                        pltpu.VMEM((2, 2, slabs, LANES), F32),
                        pltpu.VMEM((2, 2, slabs, LANES), F32),
                        pltpu.VMEM((n_pat - 1, 2, L, LANES), F32),
                        pltpu.VMEM((n_pat - 1, 2, L, LANES), F32),
                        pltpu.VMEM((2, slabs, LANES), F32),
                        pltpu.VMEM((2, slabs, LANES), F32),
                        pltpu.VMEM((2,) + bias.shape, F32),
                        pltpu.VMEM((ATT_HEADS, ATT_BLOCK, 2 * ATT_BLOCK), F32),
                        pltpu.VMEM((ATT_HEADS, ATT_BLOCK, 2 * ATT_BLOCK), F32)],
        compiler_params=_cparams(("parallel", "arbitrary")),
        name="dilated_attn",
    )(z, z, z, gq, gk, bias)


def _post_kernel(x_ref, ya_ref, yc_ref, yd_ref, val0_ref, gate0_ref, valn_ref, gaten_ref,
                 cw_ref, cb_ref, clg_ref, clb_ref, gmix_ref, wout_ref, gmlp_ref, wup_ref, wdn_ref,
                 o_ref, h_ref, yb_ref):
    conv_params = (cw_ref, cb_ref, clg_ref, clb_ref)

    @pl.when(pl.program_id(1) == 0)
    def _():
        for step in _conv_tile_steps(val0_ref, gate0_ref, *conv_params, yb_ref, h_ref, first=True):
            step()

    conv_steps = _conv_tile_steps(valn_ref, gaten_ref, *conv_params, yb_ref, h_ref, first=False)
    ys = (ya_ref[...], yb_ref[...], yc_ref[...], yd_ref[...])
    conv_steps.pop(0)()
    acc = x_ref[...]
    for j, y in enumerate(ys):
        cs = slice(j * D_GROUP, (j + 1) * D_GROUP)
        yn = y * lax.rsqrt(jnp.mean(y * y, axis=-1, keepdims=True) + EPS) * gmix_ref[:, cs]
        acc = acc + _dot(yn.astype(BF16), wout_ref[cs, :])
    h = (acc * lax.rsqrt(jnp.mean(acc * acc, axis=-1, keepdims=True) + EPS) * gmlp_ref[...]).astype(BF16)
    mlp = jnp.zeros_like(acc)
    ff_tile = D_MODEL
    per_dot = len(conv_steps) * ff_tile // (2 * D_FF)
    for f0 in range(0, D_FF, ff_tile):
        hm = jnp.maximum(_dot(h, wup_ref[:, f0:f0 + ff_tile]), 0.0)
        for _ in range(per_dot):
            conv_steps.pop(0)()
        mlp = mlp + _dot((hm * hm).astype(BF16), wdn_ref[f0:f0 + ff_tile, :])
        for _ in range(per_dot):
            conv_steps.pop(0)()
    assert not conv_steps
    o_ref[...] = acc + mlp


def _post(x, z, y_a, y_c, y_d, conv_w, conv_b, ln_g, ln_b, gmix, gmlp, wout_all, wup_all, wdn_all, layer):
    bsz, s, d = x.shape
    n_tiles = s // ROW_TILE
    row = lambda n: pl.BlockSpec((None, ROW_TILE, n), lambda b, i: (b, i, 0))
    const = lambda shape: pl.BlockSpec(shape, lambda b, i: (0, 0), pipeline_mode=pl.Buffered(1))
    weight = lambda shape: pl.BlockSpec((None,) + shape, lambda b, i: (layer, 0, 0),
                                        pipeline_mode=pl.Buffered(1))
    z_first = lambda j: pl.BlockSpec((None, ROW_TILE, D_GROUP), lambda b, i: (b, 0, j))
    z_next = lambda j: pl.BlockSpec((None, ROW_TILE, D_GROUP),
                                    lambda b, i: (b, jnp.minimum(i + 1, n_tiles - 1), j))
    vec = lambda v: v.reshape(1, -1)
    return pl.pallas_call(
        _post_kernel,
        grid=(bsz, n_tiles),
        in_specs=[row(d), row(D_GROUP), row(D_GROUP), row(D_GROUP),
                  z_first(4), z_first(5), z_next(4), z_next(5),
                  const((CONV_WIDTH, D_GROUP)), const((1, D_GROUP)), const((1, D_GROUP)), const((1, D_GROUP)),
                  const((1, d)), weight((d, d)), const((1, d)), weight((d, D_FF)), weight((D_FF, d))],
        out_specs=row(d),
        out_shape=jax.ShapeDtypeStruct((bsz, s, d), F32),
        scratch_shapes=[pltpu.VMEM((CONV_HALO + ROW_TILE, D_GROUP), F32),
                        pltpu.VMEM((ROW_TILE, D_GROUP), F32)],
        compiler_params=_cparams(("parallel", "arbitrary")),
        name="outproj_mlp",
    )(x, y_a, y_c, y_d, z, z, z, z, conv_w, vec(conv_b), vec(ln_g), vec(ln_b),
      vec(gmix), wout_all, vec(gmlp), wup_all, wdn_all)


def kernel(x, norm_mix_g, w_in, hgrn_lb_logits, hgrn_norm_g, conv_w, conv_b, conv_ln_g, conv_ln_b, s5_lambda_re, s5_lambda_im, s5_b_re, s5_b_im, s5_c_re, s5_c_im, s5_d, s5_log_dt, s5_w_glu, attn_q_norm_g, attn_k_norm_g, rel_bias, mix_out_norm_g, w_out, norm_mlp_g, w_mlp_up, w_mlp_down):
    depth = w_in.shape[0]
    lb_sm = jax.nn.softmax(hgrn_lb_logits.astype(F32), axis=0)
    lb_all = jnp.maximum(jnp.cumsum(lb_sm, axis=0) - lb_sm[0], 0.0)
    bias = _attn_bias(rel_bias)
    w_in, w_out, w_mlp_up, w_mlp_down = (w.astype(BF16) for w in (w_in, w_out, w_mlp_up, w_mlp_down))
    s5_a, s5_bm, s5_cm = jax.vmap(_s5_matrices)(s5_lambda_re, s5_lambda_im, s5_b_re, s5_b_im,
                                                 s5_c_re, s5_c_im, s5_log_dt)
    s5_w_glu = s5_w_glu.astype(BF16)
    for l in range(depth):
        z = _inproj(x, norm_mix_g[l], w_in, l)
        y_a = _hgrn(z, lb_all[l], hgrn_norm_g[l])
        y_c = _s5(z, s5_a, s5_bm, s5_cm, s5_d[l], s5_w_glu, l)
        y_d = _attn(z, attn_q_norm_g[l], attn_k_norm_g[l], bias)
        x = _post(x, z, y_a, y_c, y_d, conv_w[l], conv_b[l], conv_ln_g[l], conv_ln_b[l],
                  mix_out_norm_g[l], norm_mlp_g[l], w_out, w_mlp_up, w_mlp_down, l)
    return x
```

```python
import functools
import math

import numpy as np
import jax
import jax.numpy as jnp
from jax import lax
from jax.experimental import pallas as pl
from jax.experimental.pallas import tpu as pltpu

F32 = jnp.float32
BF16 = jnp.bfloat16

D_MODEL = 1024
D_GROUP = 256
N_MIXERS = 4
HG_HEADS = 4
HG_DK = 64
LB_FLOOR = 1e-30
CONV_WIDTH = 31
S5_CH = 16
S5_GROUPS = 16
S5_STATE = 64
ATT_HEADS = 4
ATT_DH = 64
ATT_PATTERNS = ((128, 1), (512, 4), (2048, 16))
ATT_BLOCK = 128
REL_BUCKETS = 32
REL_MAX_DIST = 2048
NEG_BIG = -1e30
D_FF = 4 * D_MODEL
N_IN_SLICES = 10
D_IN = N_IN_SLICES * D_GROUP
EPS = 1e-6
LOG2E = 1.4426950408889634

LANES = 128
SUBLANES = 8
VMEM_LIMIT = 56 * 1024 * 1024

ROW_TILE = 512
INPROJ_TILE = 1024
HG_CHUNK = 128
HG_BLOCK = 1024
CONV_SUB = 64
CONV_HALO = 32
S5_BLOCK = 512
S5_PITCH = S5_BLOCK + SUBLANES // 2
S5_TILES = 2 * S5_GROUPS * S5_STATE // LANES
ATT_L = 2048
ATT_TAIL = 512
ATT_P16 = ATT_BLOCK + SUBLANES
ATT_UNROLL = 16


def _cparams(sem):
    return pltpu.CompilerParams(dimension_semantics=sem, vmem_limit_bytes=VMEM_LIMIT)


def _sigmoid(x):
    return 1.0 / (1.0 + jnp.exp(-x))


def _dot(a, b):
    return jnp.dot(a, b, preferred_element_type=F32)


def _dot_nt(a, b):
    return lax.dot_general(a, b, (((1,), (1,)), ((), ())), preferred_element_type=F32)


def _dot_tn(a, b):
    return lax.dot_general(a, b, (((0,), (0,)), ((), ())), preferred_element_type=F32)


def _inproj_kernel(x_ref, g_ref, w_ref, z_ref):
    x = x_ref[...]
    h = x * lax.rsqrt(jnp.mean(x * x, axis=-1, keepdims=True) + EPS) * g_ref[...]
    z_ref[...] = _dot(h.astype(BF16), w_ref[...]).astype(z_ref.dtype)


def _inproj(x, g, w_all, layer):
    bsz, s, d = x.shape
    n = w_all.shape[2]
    return pl.pallas_call(
        _inproj_kernel,
        grid=(bsz, s // INPROJ_TILE),
        in_specs=[
            pl.BlockSpec((None, INPROJ_TILE, d), lambda b, i: (b, i, 0)),
            pl.BlockSpec((1, d), lambda b, i: (0, 0)),
            pl.BlockSpec((None, d, n), lambda b, i: (layer, 0, 0)),
        ],
        out_specs=pl.BlockSpec((None, INPROJ_TILE, n), lambda b, i: (b, i, 0)),
        out_shape=jax.ShapeDtypeStruct((bsz, s, n), BF16),
        compiler_params=_cparams(("parallel", "parallel")),
        name="inproj",
    )(x, g.reshape(1, d), w_all)


def _hold(gc, m):
    c, w = gc.shape
    if 2 * m >= SUBLANES:
        x = gc.reshape(c // (2 * m), 2 * m, w)
        return jnp.broadcast_to(x[:, m - 1:m, :], x.shape).reshape(c, w)
    x = gc.reshape(c // SUBLANES, SUBLANES, w)
    rows = lax.broadcasted_iota(jnp.int32, x.shape, 1)
    out = None
    for p in range(SUBLANES // (2 * m)):
        e = p * 2 * m + m - 1
        b = jnp.broadcast_to(x[:, e:e + 1, :], x.shape)
        out = b if out is None else jnp.where(rows >= p * 2 * m, b, out)
    return out.reshape(c, w)


def _hgrn_kernel(q_ref, f_ref, i_ref, g_ref, lb_ref, ng_ref, lvl_ref, o_ref, st_ref):
    @pl.when(pl.program_id(1) == 0)
    def _():
        st_ref[...] = jnp.zeros_like(st_ref)

    c = HG_CHUNK
    w = D_GROUP
    n_levels = int(math.log2(c))
    lane_head = lax.broadcasted_iota(jnp.int32, (c, w), 1) // HG_DK
    row = lax.broadcasted_iota(jnp.int32, (c, w), 0)
    blk = (lax.broadcasted_iota(jnp.int32, (w, w), 0) // HG_DK
           == lax.broadcasted_iota(jnp.int32, (w, w), 1) // HG_DK)
    ones_blk = jnp.where(blk, 1.0, 0.0).astype(BF16)
    lb = jnp.clip(lb_ref[...], 0.0, 1.0 - 1e-6)
    lb_floor = jnp.maximum(lb, LB_FLOOR)
    tri = jnp.where(lax.broadcasted_iota(jnp.int32, (c, c), 0) >= lax.broadcasted_iota(jnp.int32, (c, c), 1),
                    1.0, 0.0).astype(BF16)
    sgn = [jnp.where((row & (2 ** lv)) != 0, 1.0, -1.0) for lv in range(n_levels)]
    grp = SUBLANES

    def stack_heads(t):
        return jnp.concatenate(
            [jnp.where(lane_head == h, t, 0.0).astype(BF16) for h in range(HG_HEADS)], axis=0)

    q_all = q_ref[...].astype(F32)
    gated = (1.0 - lb) * _sigmoid(f_ref[...].astype(F32))
    lf_all = jnp.log2(lb_floor + gated)
    kk_all = (1.0 - lb) - gated
    qk = q_all * kk_all
    qk_hi = qk.astype(BF16)
    qk_lo = (qk - qk_hi.astype(F32)).astype(BF16)
    diag_all = _dot(qk_hi, ones_blk) + _dot(qk_lo, ones_blk)

    def chunk(r0, st):
        q = q_all[r0:r0 + c, :]
        kk = kk_all[r0:r0 + c, :]
        lf = lf_all[r0:r0 + c, :]
        v = i_ref[r0:r0 + c, :].astype(F32)
        hi = lf.astype(BF16)
        r1 = lf - hi.astype(F32)
        mid = r1.astype(BF16)
        lo = (r1 - mid.astype(F32)).astype(BF16)
        gc = _dot(tri, hi) + _dot(tri, mid) + _dot(tri, lo)

        q_b = q.astype(BF16)
        k_b = [jnp.where(lane_head == h, kk, 0.0).astype(BF16) for h in range(HG_HEADS)]
        n_grp = c // grp
        scores = [None] * n_grp
        for lv in range(n_levels):
            m = 2 ** lv
            e = jnp.exp2((gc - _hold(gc, m)) * sgn[lv]).astype(BF16)
            p_lv = _dot_nt(q_b * e, jnp.concatenate([kb * e for kb in k_b], axis=0))
            for gi in range(n_grp):
                if m >= grp and (gi * grp) & m == 0:
                    continue
                rs = slice(gi * grp, (gi + 1) * grp)
                old = 0.0 if scores[gi] is None else scores[gi]
                scores[gi] = jnp.where(lvl_ref[rs, :] == lv, p_lv[rs], old)
        o = _dot(jnp.concatenate(scores, axis=0).astype(BF16), stack_heads(v))
        o = o + diag_all[r0:r0 + c, :] * v
        o = o + _dot_nt((q * jnp.exp2(gc)).astype(BF16), st.astype(BF16))
        g_last = gc[c - 1:c, :]
        upd = _dot_tn(v.astype(BF16), (kk * jnp.exp2(g_last - gc)).astype(BF16))
        return o, st * jnp.exp2(g_last) + jnp.where(blk, upd, 0.0)

    st = st_ref[...]
    outs = []
    for r0 in range(0, q_ref.shape[0], c):
        o, st = chunk(r0, st)
        outs.append(o)
    st_ref[...] = st

    o_all = jnp.concatenate(outs, axis=0)
    ms = _dot((o_all * o_all).astype(BF16), ones_blk) * (1.0 / HG_DK)
    gate = g_ref[...].astype(F32)
    o_ref[...] = o_all * lax.rsqrt(ms + EPS) * ng_ref[...] * (gate * _sigmoid(gate))


def _hgrn_levels():
    t = np.arange(HG_CHUNK)[:, None]
    s = np.arange(HG_CHUNK)[None, :]
    x = np.maximum(t ^ s, 1)
    lv = np.where(s < t, np.floor(np.log2(x)).astype(np.int32), -1).astype(np.int32)
    return np.tile(lv, (1, HG_HEADS))


def _hgrn(z, lb, norm_g):
    bsz, s, _ = z.shape
    tb = min(HG_BLOCK, s)

    def zspec(j):
        return pl.BlockSpec((None, tb, D_GROUP), lambda b, i, j=j: (b, i, j))

    const = lambda shape: pl.BlockSpec(shape, lambda b, i: (0, 0))
    return pl.pallas_call(
        _hgrn_kernel,
        grid=(bsz, s // tb),
        in_specs=[zspec(0), zspec(1), zspec(2), zspec(3),
                  const((1, D_GROUP)), const((1, D_GROUP)), const((HG_CHUNK, HG_HEADS * HG_CHUNK))],
        out_specs=pl.BlockSpec((None, tb, D_GROUP), lambda b, i: (b, i, 0)),
        out_shape=jax.ShapeDtypeStruct((bsz, s, D_GROUP), F32),
        scratch_shapes=[pltpu.VMEM((D_GROUP, D_GROUP), F32)],
        compiler_params=_cparams(("parallel", "arbitrary")),
        name="hgrn2",
    )(z, z, z, z, lb.reshape(1, D_GROUP), norm_g.reshape(1, D_GROUP), jnp.asarray(_hgrn_levels()))


def _conv_tile_steps(val_ref, gate_ref, w_ref, b_ref, lg_ref, lbias_ref, o_ref, h_ref, first):
    tb = val_ref.shape[0]
    base = CONV_HALO - (CONV_WIDTH - 1)

    def glu():
        if first:
            h_ref[0:CONV_HALO, :] = jnp.zeros((CONV_HALO, D_GROUP), F32)
        else:
            h_ref[0:CONV_HALO, :] = h_ref[tb:tb + CONV_HALO, :]
        h_ref[CONV_HALO:CONV_HALO + tb, :] = val_ref[...].astype(F32) * _sigmoid(gate_ref[...].astype(F32))

    def sub_tile(t0):
        acc = jnp.broadcast_to(b_ref[...], (CONV_SUB, D_GROUP))
        for c in range(SUBLANES):
            part = None
            rows = CONV_SUB + (SUBLANES if c else 0)
            for j in range(CONV_WIDTH):
                if (base + j) % SUBLANES != c:
                    continue
                r0 = t0 + base + j - c
                term = w_ref[j:j + 1, :] * h_ref[r0:r0 + rows, :]
                part = term if part is None else part + term
            acc = acc + part[c:c + CONV_SUB, :]
        mu = jnp.mean(acc, axis=-1, keepdims=True)
        d = acc - mu
        var = jnp.mean(d * d, axis=-1, keepdims=True)
        y = d * lax.rsqrt(var + EPS) * lg_ref[...] + lbias_ref[...]
        o_ref[t0:t0 + CONV_SUB, :] = y * _sigmoid(y)

    return [glu] + [functools.partial(sub_tile, t0) for t0 in range(0, tb, CONV_SUB)]


def _s5_kernel(u_ref, a_ref, bm_ref, cm_ref, d_ref, wg_ref, o_ref, bu_ref, hs_ref, st_ref):
    nb, t_len, _ = u_ref.shape
    half = S5_TILES // 2

    @pl.when(pl.program_id(0) == 0)
    def _():
        st_ref[...] = jnp.zeros_like(st_ref)

    ar = a_ref[0:half, :]
    ai = a_ref[half:S5_TILES, :]

    for b in range(nb):
        bu = _dot(u_ref[b], bm_ref[...])
        for i in range(S5_TILES):
            bu_ref[b, i * S5_PITCH:i * S5_PITCH + t_len, :] = bu[:, i * LANES:(i + 1) * LANES]

    hs = [(st_ref[b, 0:half, :], st_ref[b, half:S5_TILES, :]) for b in range(nb)]
    for t in range(t_len):
        for b in range(nb):
            hr, hi = hs[b]
            br = bu_ref[b, pl.ds(t, half, stride=S5_PITCH), :]
            bi = bu_ref[b, pl.ds(half * S5_PITCH + t, half, stride=S5_PITCH), :]
            nr = ar * hr - ai * hi + br
            ni = ar * hi + ai * hr + bi
            hs_ref[b, pl.ds(t, half, stride=S5_PITCH), :] = nr
            hs_ref[b, pl.ds(half * S5_PITCH + t, half, stride=S5_PITCH), :] = ni
            hs[b] = (nr, ni)
    for b in range(nb):
        st_ref[b, 0:half, :] = hs[b][0]
        st_ref[b, half:S5_TILES, :] = hs[b][1]

    hall = jnp.concatenate([jnp.concatenate(
        [hs_ref[b, i * S5_PITCH:i * S5_PITCH + t_len, :].astype(BF16) for i in range(S5_TILES)], axis=1)
        for b in range(nb)], axis=0)
    uall = jnp.concatenate([u_ref[b] for b in range(nb)], axis=0).astype(F32)
    y = _dot(hall, cm_ref[...]) + d_ref[...] * uall
    y = 0.5 * y * (1.0 + jnp.tanh(math.sqrt(2.0 / math.pi) * (y + 0.044715 * (y * y * y))))
    y = y * _sigmoid(_dot(y.astype(BF16), wg_ref[...]))
    for b in range(nb):
        o_ref[b] = y[b * t_len:(b + 1) * t_len, :]


def _s5_matrices(lam_re, lam_im, b_re, b_im, c_re, c_im, log_dt):
    dt = jnp.exp(log_dt.astype(F32))[:, None]
    lr = lam_re.astype(F32)
    li = lam_im.astype(F32)
    mag = jnp.exp(lr * dt)
    a_re = mag * jnp.cos(li * dt)
    a_im = mag * jnp.sin(li * dt)
    den = lr * lr + li * li
    coef_re = ((a_re - 1.0) * lr + a_im * li) / den
    coef_im = (a_im * lr - (a_re - 1.0) * li) / den
    br = b_re.astype(F32)
    bi = b_im.astype(F32)
    bbar_re = coef_re[..., None] * br - coef_im[..., None] * bi
    bbar_im = coef_re[..., None] * bi + coef_im[..., None] * br
    eye = jnp.eye(S5_GROUPS, dtype=F32)
    n_state = S5_GROUPS * S5_STATE

    def b_block(bb):
        return jnp.einsum('gpc,gh->gchp', bb, eye).reshape(D_GROUP, n_state)

    def c_block(cc):
        return jnp.einsum('gcp,gh->gphc', cc, eye).reshape(n_state, D_GROUP)

    bm = jnp.concatenate([b_block(bbar_re), b_block(bbar_im)], axis=1).astype(BF16)
    cm = jnp.concatenate([c_block(c_re.astype(F32)), -c_block(c_im.astype(F32))], axis=0).astype(BF16)
    a = jnp.concatenate([a_re.reshape(n_state // LANES, LANES), a_im.reshape(n_state // LANES, LANES)], 0)
    return a, bm, cm


def _s5(z, a_all, bm_all, cm_all, d_skip, wglu_all, layer):
    bsz, s, _ = z.shape
    tb = min(S5_BLOCK, s)
    assert tb == S5_BLOCK
    const = lambda shape: pl.BlockSpec(shape, lambda i: (0, 0))
    per_layer = lambda shape: pl.BlockSpec((None,) + shape, lambda i: (layer, 0, 0))
    n_state2 = 2 * S5_GROUPS * S5_STATE
    return pl.pallas_call(
        _s5_kernel,
        grid=(s // tb,),
        in_specs=[pl.BlockSpec((bsz, tb, D_GROUP), lambda i: (0, i, 6)),
                  per_layer((S5_TILES, LANES)), per_layer((D_GROUP, n_state2)), per_layer((n_state2, D_GROUP)),
                  const((1, D_GROUP)), per_layer((D_GROUP, D_GROUP))],
        out_specs=pl.BlockSpec((bsz, tb, D_GROUP), lambda i: (0, i, 0)),
        out_shape=jax.ShapeDtypeStruct((bsz, s, D_GROUP), F32),
        scratch_shapes=[pltpu.VMEM((bsz, S5_TILES * S5_PITCH, LANES), F32),
                        pltpu.VMEM((bsz, S5_TILES * S5_PITCH, LANES), F32),
                        pltpu.VMEM((bsz, S5_TILES, LANES), F32)],
        compiler_params=_cparams(("arbitrary",)),
        name="s5",
    )(z, a_all, bm_all, cm_all, d_skip.reshape(1, D_GROUP), wglu_all)


def _t5_bucket_np(dist):
    max_exact = REL_BUCKETS // 2
    distf = np.maximum(dist, max_exact).astype(np.float32)
    large = max_exact + (np.log(distf / np.float32(max_exact)) / np.float32(math.log(REL_MAX_DIST / max_exact))
                         * np.float32(REL_BUCKETS - max_exact)).astype(np.int32)
    large = np.minimum(large, REL_BUCKETS - 1)
    return np.where(dist < max_exact, dist, large)


def _attn_bias(rel_bias):
    blk = ATT_BLOCK
    period = 3 * blk + 1
    k = np.arange(-(blk - 1), 2 * blk)
    tabs = []
    for window, dilation in ATT_PATTERNS:
        delta = blk - k
        valid = (delta >= 0) & (delta <= window // dilation)
        bucket = _t5_bucket_np(np.maximum(delta, 0) * dilation)
        vals = jnp.where(valid[:, None], rel_bias.astype(F32)[bucket] * LOG2E, NEG_BIG)
        v = jnp.full((period, ATT_HEADS), NEG_BIG, F32).at[k % period].set(vals).T
        tiled = jnp.tile(v, (1, blk))[:, :blk * (period - 1)]
        tabs.append(tiled.reshape(ATT_HEADS, blk, period - 1)[:, :, :2 * blk])
    return jnp.concatenate(tabs, axis=0)


def _attn_kernel(q_ref, k_ref, v_ref, qg_ref, kg_ref, bias_ref, o_ref,
                 qn_ref, kn_ref, vn_ref, q16_ref, k16_ref, v16_ref,
                 on_ref, lse_ref, on16_ref, lse16_ref, btab_ref, sc0_ref, sc1_ref):
    L = q_ref.shape[0]
    blk = ATT_BLOCK
    n_pat = len(ATT_PATTERNS)
    max_dil = ATT_PATTERNS[-1][1]
    tail = ATT_TAIL
    p16 = ATT_P16
    step = pl.program_id(1)
    first = step == 0
    cur = step % 2
    half_lane = lax.broadcasted_iota(jnp.int32, (blk, LANES), 1) < ATT_DH
    mean_blk = jnp.where(
        lax.broadcasted_iota(jnp.int32, (LANES, LANES), 0) // ATT_DH
        == lax.broadcasted_iota(jnp.int32, (LANES, LANES), 1) // ATT_DH, 1.0 / ATT_DH, 0.0).astype(BF16)

    @pl.when(first)
    def _():
        kn_ref[:, 0:tail, :] = jnp.zeros((2, tail, LANES), F32)
        vn_ref[:, 0:tail, :] = jnp.zeros((2, tail, LANES), F32)
        k16_ref[...] = jnp.zeros_like(k16_ref)
        v16_ref[...] = jnp.zeros_like(v16_ref)
        prev_cols = lax.broadcasted_iota(jnp.int32, (blk, 2 * blk), 1) < blk
        for t in range(n_pat * ATT_HEADS):
            btab_ref[0, t] = bias_ref[t]
            btab_ref[1, t] = jnp.where(prev_cols, NEG_BIG, bias_ref[t])

    @pl.when(step == 1)
    def _():
        for t in range(n_pat * ATT_HEADS):
            btab_ref[1, t] = bias_ref[t]

    @pl.when(jnp.logical_not(first))
    def _():
        kn_ref[:, 0:tail, :] = kn_ref[:, L:L + tail, :]
        vn_ref[:, 0:tail, :] = vn_ref[:, L:L + tail, :]

    def head_norm(x, g):
        ms = _dot((x * x).astype(BF16), mean_blk)
        return x * (lax.rsqrt(ms + EPS) * g)

    q_gain = qg_ref[...] * (ATT_DH ** -0.5 * LOG2E)
    for p in range(2):
        ls = slice(p * LANES, (p + 1) * LANES)
        qn_ref[p] = head_norm(q_ref[:, ls].astype(F32), q_gain)
        kn_ref[p, tail:tail + L, :] = head_norm(k_ref[:, ls].astype(F32), kg_ref[...])
        vn_ref[p, tail:tail + L, :] = v_ref[:, ls].astype(F32)

    def regroup(n, carry):
        src = pl.multiple_of(n * max_dil, max_dil)
        dst = pl.ds(n, max_dil, stride=p16)
        for p in range(2):
            q16_ref[p, dst, :] = qn_ref[p, pl.ds(src, max_dil), :]
            k16_ref[p, cur, dst, :] = kn_ref[p, pl.ds(tail + src, max_dil), :]
            v16_ref[p, cur, dst, :] = vn_ref[p, pl.ds(tail + src, max_dil), :]
        return carry

    lax.fori_loop(0, L // max_dil, regroup, 0, unroll=4)

    def rows(start, n, dil):
        return pl.ds(start, n) if dil == 1 else pl.ds(start, n, stride=dil)

    def q_rows(ci, dil):
        return (ci % dil) + dil * blk * (ci // dil)

    def slab_rows(ci):
        r0 = ci * p16
        return pl.ds(r0 if isinstance(r0, int) else pl.multiple_of(r0, SUBLANES), blk)

    def load_q(pi, p, ci):
        dil = ATT_PATTERNS[pi][1]
        if dil == max_dil:
            return q16_ref[p, slab_rows(ci), :]
        return qn_ref[p, rows(q_rows(ci, dil), blk, dil), :]

    def load_kv(nat_ref, slab_ref, pi, p, ci):
        dil = ATT_PATTERNS[pi][1]
        if dil == max_dil:
            return jnp.concatenate([slab_ref[p, 1 - cur, slab_rows(ci), :], slab_ref[p, cur, slab_rows(ci), :]],
                                   axis=0)
        return nat_ref[p, rows(tail + q_rows(ci, dil) - dil * blk, 2 * blk, dil), :]

    def store_out(pi, p, ci, on, lse):
        dil = ATT_PATTERNS[pi][1]
        if dil == max_dil:
            on16_ref[p, slab_rows(ci), :] = on
            lse16_ref[p, slab_rows(ci), :] = lse
        else:
            on_ref[pi, p, rows(q_rows(ci, dil), blk, dil), :] = on
            lse_ref[pi, p, rows(q_rows(ci, dil), blk, dil), :] = lse

    ones_cols = jnp.ones((2 * blk, LANES), BF16)
    n_combo = L // blk

    def scores(pi, ci, s_ref):
        tsel = jnp.where(ci < ATT_PATTERNS[pi][1], 1, 0)
        for p in range(2):
            qb = load_q(pi, p, ci)
            kb = load_kv(kn_ref, k16_ref, pi, p, ci).astype(BF16)
            for hh in range(2):
                sel = half_lane if hh == 0 else jnp.logical_not(half_lane)
                h = 2 * p + hh
                s_ref[h] = (_dot_nt(jnp.where(sel, qb, 0.0).astype(BF16), kb)
                            + btab_ref[tsel, pi * ATT_HEADS + h])

    def softmax_pv(pi, ci, s_ref):
        for p in range(2):
            vb = jnp.concatenate([load_kv(vn_ref, v16_ref, pi, p, ci).astype(BF16), ones_cols], axis=1)
            mxs, ovs = [], []
            for hh in range(2):
                sc = s_ref[2 * p + hh]
                mx = jnp.max(sc, axis=-1, keepdims=True)
                ovs.append(_dot(jnp.exp2(sc - mx).astype(BF16), vb))
                mxs.append(mx)
            den = jnp.where(half_lane, ovs[0][:, LANES:], ovs[1][:, LANES:])
            num = jnp.where(half_lane, ovs[0][:, :LANES], ovs[1][:, :LANES])
            store_out(pi, p, ci, num * (1.0 / den), jnp.where(half_lane, mxs[0], mxs[1]) + jnp.log2(den))

    sc_refs = (sc0_ref, sc1_ref)
    unroll = ATT_UNROLL
    scores(0, 0, sc0_ref)
    for pi in range(n_pat):
        def group(c0, last, pi=pi):
            for k in range(unroll):
                if not last or k + 1 < unroll:
                    scores(pi, c0 + k + 1, sc_refs[(k + 1) % 2])
                elif pi + 1 < n_pat:
                    scores(pi + 1, 0, sc_refs[(k + 1) % 2])
                softmax_pv(pi, c0 + k, sc_refs[k % 2])

        if n_combo > unroll:
            def body(j, carry, group=group):
                group(j * unroll, False)
                return carry

            lax.fori_loop(0, n_combo // unroll - 1, body, 0)
        group(n_combo - unroll, True)

    def merge(ti, carry):
        t0 = pl.multiple_of(ti * blk, blk)
        g0 = ti * (blk // max_dil)

        def regrouped(ref, p):
            return jnp.concatenate([ref[p, pl.ds(g0 + g, max_dil, stride=p16), :]
                                    for g in range(blk // max_dil)], axis=0)

        for p in range(2):
            ls_ = [lse_ref[pi, p, pl.ds(t0, blk), :] for pi in range(n_pat - 1)] + [regrouped(lse16_ref, p)]
            os_ = [on_ref[pi, p, pl.ds(t0, blk), :] for pi in range(n_pat - 1)] + [regrouped(on16_ref, p)]
            mx = functools.reduce(jnp.maximum, ls_)
            ws = [jnp.exp2(x - mx) for x in ls_]
            num = sum(w * o for w, o in zip(ws, os_))
            o_ref[pl.ds(t0, blk), p * LANES:(p + 1) * LANES] = num / sum(ws)
        return carry

    lax.fori_loop(0, L // blk, merge, 0)


def _attn(z, q_g, k_g, bias):
    bsz, s, _ = z.shape
    L = ATT_L
    assert s % L == 0
    gq = jnp.concatenate([q_g, q_g]).reshape(1, LANES).astype(F32)
    gk = jnp.concatenate([k_g, k_g]).reshape(1, LANES).astype(F32)
    n_pat = len(ATT_PATTERNS)
    slabs = ATT_PATTERNS[-1][1] * ATT_P16

    def zspec(j):
        return pl.BlockSpec((None, L, D_GROUP), lambda b, i, j=j: (b, i, j))

    return pl.pallas_call(
        _attn_kernel,
        grid=(bsz, s // L),
        in_specs=[zspec(7), zspec(8), zspec(9),
                  pl.BlockSpec((1, LANES), lambda b, i: (0, 0)),
                  pl.BlockSpec((1, LANES), lambda b, i: (0, 0)),
                  pl.BlockSpec(bias.shape, lambda b, i: (0, 0, 0))],
        out_specs=pl.BlockSpec((None, L, D_GROUP), lambda b, i: (b, i, 0)),
        out_shape=jax.ShapeDtypeStruct((bsz, s, D_GROUP), F32),
        scratch_shapes=[pltpu.VMEM((2, L, LANES), F32),
                        pltpu.VMEM((2, ATT_TAIL + L, LANES), F32),
                        pltpu.VMEM((2, ATT_TAIL + L, LANES), F32),
                        pltpu.VMEM((2, slabs, LANES), F32),
                        pltpu.VMEM((2, 2, slabs, LANES), F32),
                        pltpu.VMEM((2, 2, slabs, LANES), F32),
                        pltpu.VMEM((n_pat - 1, 2, L, LANES), F32),
                        pltpu.VMEM((n_pat - 1, 2, L, LANES), F32),
                        pltpu.VMEM((2, slabs, LANES), F32),
                        pltpu.VMEM((2, slabs, LANES), F32),
                        pltpu.VMEM((2,) + bias.shape, F32),
                        pltpu.VMEM((ATT_HEADS, ATT_BLOCK, 2 * ATT_BLOCK), F32),
                        pltpu.VMEM((ATT_HEADS, ATT_BLOCK, 2 * ATT_BLOCK), F32)],
        compiler_params=_cparams(("parallel", "arbitrary")),
        name="dilated_attn",
    )(z, z, z, gq, gk, bias)


def _post_kernel(x_ref, ya_ref, yc_ref, yd_ref, val0_ref, gate0_ref, valn_ref, gaten_ref,
                 cw_ref, cb_ref, clg_ref, clb_ref, gmix_ref, wout_ref, gmlp_ref, wup_ref, wdn_ref,
                 o_ref, h_ref, yb_ref):
    conv_params = (cw_ref, cb_ref, clg_ref, clb_ref)

    @pl.when(pl.program_id(1) == 0)
    def _():
        for step in _conv_tile_steps(val0_ref, gate0_ref, *conv_params, yb_ref, h_ref, first=True):
            step()

    conv_steps = _conv_tile_steps(valn_ref, gaten_ref, *conv_params, yb_ref, h_ref, first=False)
    ys = (ya_ref[...], yb_ref[...], yc_ref[...], yd_ref[...])
    conv_steps.pop(0)()
    acc = x_ref[...]
    for j, y in enumerate(ys):
        cs = slice(j * D_GROUP, (j + 1) * D_GROUP)
        yn = y * lax.rsqrt(jnp.mean(y * y, axis=-1, keepdims=True) + EPS) * gmix_ref[:, cs]
        acc = acc + _dot(yn.astype(BF16), wout_ref[cs, :])
    h = (acc * lax.rsqrt(jnp.mean(acc * acc, axis=-1, keepdims=True) + EPS) * gmlp_ref[...]).astype(BF16)
    mlp = jnp.zeros_like(acc)
    ff_tile = D_MODEL
    per_dot = len(conv_steps) * ff_tile // (2 * D_FF)
    for f0 in range(0, D_FF, ff_tile):
        hm = jnp.maximum(_dot(h, wup_ref[:, f0:f0 + ff_tile]), 0.0)
        for _ in range(per_dot):
            conv_steps.pop(0)()
        mlp = mlp + _dot((hm * hm).astype(BF16), wdn_ref[f0:f0 + ff_tile, :])
        for _ in range(per_dot):
            conv_steps.pop(0)()
    assert not conv_steps
    o_ref[...] = acc + mlp


def _post(x, z, y_a, y_c, y_d, conv_w, conv_b, ln_g, ln_b, gmix, gmlp, wout_all, wup_all, wdn_all, layer):
    bsz, s, d = x.shape
    n_tiles = s // ROW_TILE
    row = lambda n: pl.BlockSpec((None, ROW_TILE, n), lambda b, i: (b, i, 0))
    const = lambda shape: pl.BlockSpec(shape, lambda b, i: (0, 0), pipeline_mode=pl.Buffered(1))
    weight = lambda shape: pl.BlockSpec((None,) + shape, lambda b, i: (layer, 0, 0),
                                        pipeline_mode=pl.Buffered(1))
    z_first = lambda j: pl.BlockSpec((None, ROW_TILE, D_GROUP), lambda b, i: (b, 0, j))
    z_next = lambda j: pl.BlockSpec((None, ROW_TILE, D_GROUP),
                                    lambda b, i: (b, jnp.minimum(i + 1, n_tiles - 1), j))
    vec = lambda v: v.reshape(1, -1)
    return pl.pallas_call(
        _post_kernel,
        grid=(bsz, n_tiles),
        in_specs=[row(d), row(D_GROUP), row(D_GROUP), row(D_GROUP),
                  z_first(4), z_first(5), z_next(4), z_next(5),
                  const((CONV_WIDTH, D_GROUP)), const((1, D_GROUP)), const((1, D_GROUP)), const((1, D_GROUP)),
                  const((1, d)), weight((d, d)), const((1, d)), weight((d, D_FF)), weight((D_FF, d))],
        out_specs=row(d),
        out_shape=jax.ShapeDtypeStruct((bsz, s, d), F32),
        scratch_shapes=[pltpu.VMEM((CONV_HALO + ROW_TILE, D_GROUP), F32),
                        pltpu.VMEM((ROW_TILE, D_GROUP), F32)],
        compiler_params=_cparams(("parallel", "arbitrary")),
        name="outproj_mlp",
    )(x, y_a, y_c, y_d, z, z, z, z, conv_w, vec(conv_b), vec(ln_g), vec(ln_b),
      vec(gmix), wout_all, vec(gmlp), wup_all, wdn_all)


def kernel(x, norm_mix_g, w_in, hgrn_lb_logits, hgrn_norm_g, conv_w, conv_b, conv_ln_g, conv_ln_b, s5_lambda_re, s5_lambda_im, s5_b_re, s5_b_im, s5_c_re, s5_c_im, s5_d, s5_log_dt, s5_w_glu, attn_q_norm_g, attn_k_norm_g, rel_bias, mix_out_norm_g, w_out, norm_mlp_g, w_mlp_up, w_mlp_down):
    depth = w_in.shape[0]
    lb_sm = jax.nn.softmax(hgrn_lb_logits.astype(F32), axis=0)
    lb_all = jnp.maximum(jnp.cumsum(lb_sm, axis=0) - lb_sm[0], 0.0)
    bias = _attn_bias(rel_bias)
    w_in, w_out, w_mlp_up, w_mlp_down = (w.astype(BF16) for w in (w_in, w_out, w_mlp_up, w_mlp_down))
    s5_a, s5_bm, s5_cm = jax.vmap(_s5_matrices)(s5_lambda_re, s5_lambda_im, s5_b_re, s5_b_im,
                                                 s5_c_re, s5_c_im, s5_log_dt)
    s5_w_glu = s5_w_glu.astype(BF16)
    for l in range(depth):
        z = _inproj(x, norm_mix_g[l], w_in, l)
        y_a = _hgrn(z, lb_all[l], hgrn_norm_g[l])
        y_c = _s5(z, s5_a, s5_bm, s5_cm, s5_d[l], s5_w_glu, l)
        y_d = _attn(z, attn_q_norm_g[l], attn_k_norm_g[l], bias)
        x = _post(x, z, y_a, y_c, y_d, conv_w[l], conv_b[l], conv_ln_g[l], conv_ln_b[l],
                  mix_out_norm_g[l], norm_mlp_g[l], w_out, w_mlp_up, w_mlp_down, l)
    return x
```

```python
import functools
import math

import numpy as np
import jax
import jax.numpy as jnp
from jax import lax
from jax.experimental import pallas as pl
from jax.experimental.pallas import tpu as pltpu

F32 = jnp.float32
BF16 = jnp.bfloat16

D_MODEL = 1024
D_GROUP = 256
N_MIXERS = 4
HG_HEADS = 4
HG_DK = 64
LB_FLOOR = 1e-30
CONV_WIDTH = 31
S5_CH = 16
S5_GROUPS = 16
S5_STATE = 64
ATT_HEADS = 4
ATT_DH = 64
ATT_PATTERNS = ((128, 1), (512, 4), (2048, 16))
ATT_BLOCK = 128
REL_BUCKETS = 32
REL_MAX_DIST = 2048
NEG_BIG = -1e30
D_FF = 4 * D_MODEL
N_IN_SLICES = 10
D_IN = N_IN_SLICES * D_GROUP
EPS = 1e-6
LOG2E = 1.4426950408889634

LANES = 128
SUBLANES = 8
VMEM_LIMIT = 56 * 1024 * 1024

ROW_TILE = 512
INPROJ_TILE = 1024
HG_CHUNK = 128
HG_BLOCK = 2048
CONV_SUB = 64
CONV_HALO = 32
S5_BLOCK = 512
S5_PITCH = S5_BLOCK + SUBLANES // 2
S5_TILES = 2 * S5_GROUPS * S5_STATE // LANES
ATT_L = 2048
ATT_TAIL = 512
ATT_P16 = ATT_BLOCK + SUBLANES
ATT_UNROLL = 16


def _cparams(sem):
    return pltpu.CompilerParams(dimension_semantics=sem, vmem_limit_bytes=VMEM_LIMIT)


def _sigmoid(x):
    return 1.0 / (1.0 + jnp.exp(-x))


def _dot(a, b):
    return jnp.dot(a, b, preferred_element_type=F32)


def _dot_nt(a, b):
    return lax.dot_general(a, b, (((1,), (1,)), ((), ())), preferred_element_type=F32)


def _dot_tn(a, b):
    return lax.dot_general(a, b, (((0,), (0,)), ((), ())), preferred_element_type=F32)


def _inproj_kernel(x_ref, g_ref, w_ref, z_ref):
    x = x_ref[...]
    h = x * lax.rsqrt(jnp.mean(x * x, axis=-1, keepdims=True) + EPS) * g_ref[...]
    z_ref[...] = _dot(h.astype(BF16), w_ref[...]).astype(z_ref.dtype)


def _inproj(x, g, w_all, layer):
    bsz, s, d = x.shape
    n = w_all.shape[2]
    return pl.pallas_call(
        _inproj_kernel,
        grid=(bsz, s // INPROJ_TILE),
        in_specs=[
            pl.BlockSpec((None, INPROJ_TILE, d), lambda b, i: (b, i, 0)),
            pl.BlockSpec((1, d), lambda b, i: (0, 0)),
            pl.BlockSpec((None, d, n), lambda b, i: (layer, 0, 0)),
        ],
        out_specs=pl.BlockSpec((None, INPROJ_TILE, n), lambda b, i: (b, i, 0)),
        out_shape=jax.ShapeDtypeStruct((bsz, s, n), BF16),
        compiler_params=_cparams(("parallel", "parallel")),
        name="inproj",
    )(x, g.reshape(1, d), w_all)


def _hold(gc, m):
    c, w = gc.shape
    if 2 * m >= SUBLANES:
        x = gc.reshape(c // (2 * m), 2 * m, w)
        return jnp.broadcast_to(x[:, m - 1:m, :], x.shape).reshape(c, w)
    x = gc.reshape(c // SUBLANES, SUBLANES, w)
    rows = lax.broadcasted_iota(jnp.int32, x.shape, 1)
    out = None
    for p in range(SUBLANES // (2 * m)):
        e = p * 2 * m + m - 1
        b = jnp.broadcast_to(x[:, e:e + 1, :], x.shape)
        out = b if out is None else jnp.where(rows >= p * 2 * m, b, out)
    return out.reshape(c, w)


def _hgrn_kernel(q_ref, f_ref, i_ref, g_ref, lb_ref, ng_ref, lvl_ref, o_ref, st_ref):
    @pl.when(pl.program_id(1) == 0)
    def _():
        st_ref[...] = jnp.zeros_like(st_ref)

    c = HG_CHUNK
    w = D_GROUP
    n_levels = int(math.log2(c))
    lane_head = lax.broadcasted_iota(jnp.int32, (c, w), 1) // HG_DK
    row = lax.broadcasted_iota(jnp.int32, (c, w), 0)
    blk = (lax.broadcasted_iota(jnp.int32, (w, w), 0) // HG_DK
           == lax.broadcasted_iota(jnp.int32, (w, w), 1) // HG_DK)
    ones_blk = jnp.where(blk, 1.0, 0.0).astype(BF16)
    lb = jnp.clip(lb_ref[...], 0.0, 1.0 - 1e-6)
    lb_floor = jnp.maximum(lb, LB_FLOOR)
    tri = jnp.where(lax.broadcasted_iota(jnp.int32, (c, c), 0) >= lax.broadcasted_iota(jnp.int32, (c, c), 1),
                    1.0, 0.0).astype(BF16)
    sgn = [jnp.where((row & (2 ** lv)) != 0, 1.0, -1.0) for lv in range(n_levels)]
    grp = SUBLANES

    def stack_heads(t):
        return jnp.concatenate(
            [jnp.where(lane_head == h, t, 0.0).astype(BF16) for h in range(HG_HEADS)], axis=0)

    q_all = q_ref[...].astype(F32)
    gated = (1.0 - lb) * _sigmoid(f_ref[...].astype(F32))
    lf_all = jnp.log2(lb_floor + gated)
    kk_all = (1.0 - lb) - gated
    qk = q_all * kk_all
    qk_hi = qk.astype(BF16)
    qk_lo = (qk - qk_hi.astype(F32)).astype(BF16)
    diag_all = _dot(qk_hi, ones_blk) + _dot(qk_lo, ones_blk)
    hi = lf_all.astype(BF16)
    r1 = lf_all - hi.astype(F32)
    mid = r1.astype(BF16)
    lo = (r1 - mid.astype(F32)).astype(BF16)
    n_chunks = q_ref.shape[0] // c
    cums = _dot(tri, jnp.concatenate([part[j * c:(j + 1) * c, :] for j in range(n_chunks)
                                      for part in (hi, mid, lo)], axis=1))

    def chunk(r0, st):
        q = q_all[r0:r0 + c, :]
        kk = kk_all[r0:r0 + c, :]
        v = i_ref[r0:r0 + c, :].astype(F32)
        g0 = (r0 // c) * 3 * w
        gc = cums[:, g0:g0 + w] + cums[:, g0 + w:g0 + 2 * w] + cums[:, g0 + 2 * w:g0 + 3 * w]

        q_b = q.astype(BF16)
        k_b = [jnp.where(lane_head == h, kk, 0.0).astype(BF16) for h in range(HG_HEADS)]
        n_grp = c // grp
        scores = [None] * n_grp
        for lv in range(n_levels):
            m = 2 ** lv
            e = jnp.exp2((gc - _hold(gc, m)) * sgn[lv]).astype(BF16)
            p_lv = _dot_nt(q_b * e, jnp.concatenate([kb * e for kb in k_b], axis=0))
            for gi in range(n_grp):
                if m >= grp and (gi * grp) & m == 0:
                    continue
                rs = slice(gi * grp, (gi + 1) * grp)
                old = 0.0 if scores[gi] is None else scores[gi]
                scores[gi] = jnp.where(lvl_ref[rs, :] == lv, p_lv[rs], old)
        o = _dot(jnp.concatenate(scores, axis=0).astype(BF16), stack_heads(v))
        o = o + diag_all[r0:r0 + c, :] * v
        o = o + _dot_nt((q * jnp.exp2(gc)).astype(BF16), st.astype(BF16))
        g_last = gc[c - 1:c, :]
        upd = _dot_tn(v.astype(BF16), (kk * jnp.exp2(g_last - gc)).astype(BF16))
        return o, st * jnp.exp2(g_last) + jnp.where(blk, upd, 0.0)

    st = st_ref[...]
    outs = []
    for r0 in range(0, q_ref.shape[0], c):
        o, st = chunk(r0, st)
        outs.append(o)
    st_ref[...] = st

    o_all = jnp.concatenate(outs, axis=0)
    ms = _dot((o_all * o_all).astype(BF16), ones_blk) * (1.0 / HG_DK)
    gate = g_ref[...].astype(F32)
    o_ref[...] = o_all * lax.rsqrt(ms + EPS) * ng_ref[...] * (gate * _sigmoid(gate))


def _hgrn_levels():
    t = np.arange(HG_CHUNK)[:, None]
    s = np.arange(HG_CHUNK)[None, :]
    x = np.maximum(t ^ s, 1)
    lv = np.where(s < t, np.floor(np.log2(x)).astype(np.int32), -1).astype(np.int32)
    return np.tile(lv, (1, HG_HEADS))


def _hgrn(z, lb, norm_g):
    bsz, s, _ = z.shape
    tb = min(HG_BLOCK, s)

    def zspec(j):
        return pl.BlockSpec((None, tb, D_GROUP), lambda b, i, j=j: (b, i, j))

    const = lambda shape: pl.BlockSpec(shape, lambda b, i: (0, 0))
    return pl.pallas_call(
        _hgrn_kernel,
        grid=(bsz, s // tb),
        in_specs=[zspec(0), zspec(1), zspec(2), zspec(3),
                  const((1, D_GROUP)), const((1, D_GROUP)), const((HG_CHUNK, HG_HEADS * HG_CHUNK))],
        out_specs=pl.BlockSpec((None, tb, D_GROUP), lambda b, i: (b, i, 0)),
        out_shape=jax.ShapeDtypeStruct((bsz, s, D_GROUP), F32),
        scratch_shapes=[pltpu.VMEM((D_GROUP, D_GROUP), F32)],
        compiler_params=_cparams(("parallel", "arbitrary")),
        name="hgrn2",
    )(z, z, z, z, lb.reshape(1, D_GROUP), norm_g.reshape(1, D_GROUP), jnp.asarray(_hgrn_levels()))


def _conv_tile_steps(val_ref, gate_ref, w_ref, b_ref, lg_ref, lbias_ref, o_ref, h_ref, first):
    tb = val_ref.shape[0]
    base = CONV_HALO - (CONV_WIDTH - 1)

    def glu():
        if first:
            h_ref[0:CONV_HALO, :] = jnp.zeros((CONV_HALO, D_GROUP), F32)
        else:
            h_ref[0:CONV_HALO, :] = h_ref[tb:tb + CONV_HALO, :]
        h_ref[CONV_HALO:CONV_HALO + tb, :] = val_ref[...].astype(F32) * _sigmoid(gate_ref[...].astype(F32))

    def sub_tile(t0):
        acc = jnp.broadcast_to(b_ref[...], (CONV_SUB, D_GROUP))
        for c in range(SUBLANES):
            part = None
            rows = CONV_SUB + (SUBLANES if c else 0)
            for j in range(CONV_WIDTH):
                if (base + j) % SUBLANES != c:
                    continue
                r0 = t0 + base + j - c
                term = w_ref[j:j + 1, :] * h_ref[r0:r0 + rows, :]
                part = term if part is None else part + term
            acc = acc + part[c:c + CONV_SUB, :]
        mu = jnp.mean(acc, axis=-1, keepdims=True)
        d = acc - mu
        var = jnp.mean(d * d, axis=-1, keepdims=True)
        y = d * lax.rsqrt(var + EPS) * lg_ref[...] + lbias_ref[...]
        o_ref[t0:t0 + CONV_SUB, :] = y * _sigmoid(y)

    return [glu] + [functools.partial(sub_tile, t0) for t0 in range(0, tb, CONV_SUB)]


def _s5_kernel(u_ref, a_ref, bm_ref, cm_ref, d_ref, wg_ref, o_ref, bu_ref, hs_ref, st_ref):
    nb, t_len, _ = u_ref.shape
    half = S5_TILES // 2

    @pl.when(pl.program_id(0) == 0)
    def _():
        st_ref[...] = jnp.zeros_like(st_ref)

    ar = a_ref[0:half, :]
    ai = a_ref[half:S5_TILES, :]

    for b in range(nb):
        bu = _dot(u_ref[b], bm_ref[...])
        for i in range(S5_TILES):
            bu_ref[b, i * S5_PITCH:i * S5_PITCH + t_len, :] = bu[:, i * LANES:(i + 1) * LANES]

    hs = [(st_ref[b, 0:half, :], st_ref[b, half:S5_TILES, :]) for b in range(nb)]
    for t in range(t_len):
        for b in range(nb):
            hr, hi = hs[b]
            br = bu_ref[b, pl.ds(t, half, stride=S5_PITCH), :]
            bi = bu_ref[b, pl.ds(half * S5_PITCH + t, half, stride=S5_PITCH), :]
            nr = ar * hr - ai * hi + br
            ni = ar * hi + ai * hr + bi
            hs_ref[b, pl.ds(t, half, stride=S5_PITCH), :] = nr
            hs_ref[b, pl.ds(half * S5_PITCH + t, half, stride=S5_PITCH), :] = ni
            hs[b] = (nr, ni)
    for b in range(nb):
        st_ref[b, 0:half, :] = hs[b][0]
        st_ref[b, half:S5_TILES, :] = hs[b][1]

    hall = jnp.concatenate([jnp.concatenate(
        [hs_ref[b, i * S5_PITCH:i * S5_PITCH + t_len, :].astype(BF16) for i in range(S5_TILES)], axis=1)
        for b in range(nb)], axis=0)
    uall = jnp.concatenate([u_ref[b] for b in range(nb)], axis=0).astype(F32)
    y = _dot(hall, cm_ref[...]) + d_ref[...] * uall
    y = 0.5 * y * (1.0 + jnp.tanh(math.sqrt(2.0 / math.pi) * (y + 0.044715 * (y * y * y))))
    y = y * _sigmoid(_dot(y.astype(BF16), wg_ref[...]))
    for b in range(nb):
        o_ref[b] = y[b * t_len:(b + 1) * t_len, :]


def _s5_matrices(lam_re, lam_im, b_re, b_im, c_re, c_im, log_dt):
    dt = jnp.exp(log_dt.astype(F32))[:, None]
    lr = lam_re.astype(F32)
    li = lam_im.astype(F32)
    mag = jnp.exp(lr * dt)
    a_re = mag * jnp.cos(li * dt)
    a_im = mag * jnp.sin(li * dt)
    den = lr * lr + li * li
    coef_re = ((a_re - 1.0) * lr + a_im * li) / den
    coef_im = (a_im * lr - (a_re - 1.0) * li) / den
    br = b_re.astype(F32)
    bi = b_im.astype(F32)
    bbar_re = coef_re[..., None] * br - coef_im[..., None] * bi
    bbar_im = coef_re[..., None] * bi + coef_im[..., None] * br
    eye = jnp.eye(S5_GROUPS, dtype=F32)
    n_state = S5_GROUPS * S5_STATE

    def b_block(bb):
        return jnp.einsum('gpc,gh->gchp', bb, eye).reshape(D_GROUP, n_state)

    def c_block(cc):
        return jnp.einsum('gcp,gh->gphc', cc, eye).reshape(n_state, D_GROUP)

    bm = jnp.concatenate([b_block(bbar_re), b_block(bbar_im)], axis=1).astype(BF16)
    cm = jnp.concatenate([c_block(c_re.astype(F32)), -c_block(c_im.astype(F32))], axis=0).astype(BF16)
    a = jnp.concatenate([a_re.reshape(n_state // LANES, LANES), a_im.reshape(n_state // LANES, LANES)], 0)
    return a, bm, cm


def _s5(z, a_all, bm_all, cm_all, d_skip, wglu_all, layer):
    bsz, s, _ = z.shape
    tb = min(S5_BLOCK, s)
    assert tb == S5_BLOCK
    const = lambda shape: pl.BlockSpec(shape, lambda i: (0, 0))
    per_layer = lambda shape: pl.BlockSpec((None,) + shape, lambda i: (layer, 0, 0))
    n_state2 = 2 * S5_GROUPS * S5_STATE
    return pl.pallas_call(
        _s5_kernel,
        grid=(s // tb,),
        in_specs=[pl.BlockSpec((bsz, tb, D_GROUP), lambda i: (0, i, 6)),
                  per_layer((S5_TILES, LANES)), per_layer((D_GROUP, n_state2)), per_layer((n_state2, D_GROUP)),
                  const((1, D_GROUP)), per_layer((D_GROUP, D_GROUP))],
        out_specs=pl.BlockSpec((bsz, tb, D_GROUP), lambda i: (0, i, 0)),
        out_shape=jax.ShapeDtypeStruct((bsz, s, D_GROUP), F32),
        scratch_shapes=[pltpu.VMEM((bsz, S5_TILES * S5_PITCH, LANES), F32),
                        pltpu.VMEM((bsz, S5_TILES * S5_PITCH, LANES), F32),
                        pltpu.VMEM((bsz, S5_TILES, LANES), F32)],
        compiler_params=_cparams(("arbitrary",)),
        name="s5",
    )(z, a_all, bm_all, cm_all, d_skip.reshape(1, D_GROUP), wglu_all)


def _t5_bucket_np(dist):
    max_exact = REL_BUCKETS // 2
    distf = np.maximum(dist, max_exact).astype(np.float32)
    large = max_exact + (np.log(distf / np.float32(max_exact)) / np.float32(math.log(REL_MAX_DIST / max_exact))
                         * np.float32(REL_BUCKETS - max_exact)).astype(np.int32)
    large = np.minimum(large, REL_BUCKETS - 1)
    return np.where(dist < max_exact, dist, large)


def _attn_bias(rel_bias):
    blk = ATT_BLOCK
    period = 3 * blk + 1
    k = np.arange(-(blk - 1), 2 * blk)
    tabs = []
    for window, dilation in ATT_PATTERNS:
        delta = blk - k
        valid = (delta >= 0) & (delta <= window // dilation)
        bucket = _t5_bucket_np(np.maximum(delta, 0) * dilation)
        vals = jnp.where(valid[:, None], rel_bias.astype(F32)[bucket] * LOG2E, NEG_BIG)
        v = jnp.full((period, ATT_HEADS), NEG_BIG, F32).at[k % period].set(vals).T
        tiled = jnp.tile(v, (1, blk))[:, :blk * (period - 1)]
        tabs.append(tiled.reshape(ATT_HEADS, blk, period - 1)[:, :, :2 * blk])
    return jnp.concatenate(tabs, axis=0)


def _attn_kernel(q_ref, k_ref, v_ref, qg_ref, kg_ref, bias_ref, o_ref,
                 qn_ref, kn_ref, vn_ref, q16_ref, k16_ref, v16_ref,
                 on_ref, lse_ref, on16_ref, lse16_ref, btab_ref, sc0_ref, sc1_ref):
    L = q_ref.shape[0]
    blk = ATT_BLOCK
    n_pat = len(ATT_PATTERNS)
    max_dil = ATT_PATTERNS[-1][1]
    tail = ATT_TAIL
    p16 = ATT_P16
    step = pl.program_id(1)
    first = step == 0
    cur = step % 2
    half_lane = lax.broadcasted_iota(jnp.int32, (blk, LANES), 1) < ATT_DH
    mean_blk = jnp.where(
        lax.broadcasted_iota(jnp.int32, (LANES, LANES), 0) // ATT_DH
        == lax.broadcasted_iota(jnp.int32, (LANES, LANES), 1) // ATT_DH, 1.0 / ATT_DH, 0.0).astype(BF16)

    @pl.when(first)
    def _():
        kn_ref[:, 0:tail, :] = jnp.zeros((2, tail, LANES), F32)
        vn_ref[:, 0:tail, :] = jnp.zeros((2, tail, LANES), F32)
        k16_ref[...] = jnp.zeros_like(k16_ref)
        v16_ref[...] = jnp.zeros_like(v16_ref)
        prev_cols = lax.broadcasted_iota(jnp.int32, (blk, 2 * blk), 1) < blk
        for t in range(n_pat * ATT_HEADS):
            btab_ref[0, t] = bias_ref[t]
            btab_ref[1, t] = jnp.where(prev_cols, NEG_BIG, bias_ref[t])

    @pl.when(step == 1)
    def _():
        for t in range(n_pat * ATT_HEADS):
            btab_ref[1, t] = bias_ref[t]

    @pl.when(jnp.logical_not(first))
    def _():
        kn_ref[:, 0:tail, :] = kn_ref[:, L:L + tail, :]
        vn_ref[:, 0:tail, :] = vn_ref[:, L:L + tail, :]

    def head_norm(x, g):
        ms = _dot((x * x).astype(BF16), mean_blk)
        return x * (lax.rsqrt(ms + EPS) * g)

    q_gain = qg_ref[...] * (ATT_DH ** -0.5 * LOG2E)
    for p in range(2):
        ls = slice(p * LANES, (p + 1) * LANES)
        qn_ref[p] = head_norm(q_ref[:, ls].astype(F32), q_gain)
        kn_ref[p, tail:tail + L, :] = head_norm(k_ref[:, ls].astype(F32), kg_ref[...])
        vn_ref[p, tail:tail + L, :] = v_ref[:, ls].astype(F32)

    def regroup(n, carry):
        src = pl.multiple_of(n * max_dil, max_dil)
        dst = pl.ds(n, max_dil, stride=p16)
        for p in range(2):
            q16_ref[p, dst, :] = qn_ref[p, pl.ds(src, max_dil), :]
            k16_ref[p, cur, dst, :] = kn_ref[p, pl.ds(tail + src, max_dil), :]
            v16_ref[p, cur, dst, :] = vn_ref[p, pl.ds(tail + src, max_dil), :]
        return carry

    lax.fori_loop(0, L // max_dil, regroup, 0, unroll=4)

    def rows(start, n, dil):
        return pl.ds(start, n) if dil == 1 else pl.ds(start, n, stride=dil)

    def q_rows(ci, dil):
        return (ci % dil) + dil * blk * (ci // dil)

    def slab_rows(ci):
        r0 = ci * p16
        return pl.ds(r0 if isinstance(r0, int) else pl.multiple_of(r0, SUBLANES), blk)

    def load_q(pi, p, ci):
        dil = ATT_PATTERNS[pi][1]
        if dil == max_dil:
            return q16_ref[p, slab_rows(ci), :]
        return qn_ref[p, rows(q_rows(ci, dil), blk, dil), :]

    def load_kv(nat_ref, slab_ref, pi, p, ci):
        dil = ATT_PATTERNS[pi][1]
        if dil == max_dil:
            return jnp.concatenate([slab_ref[p, 1 - cur, slab_rows(ci), :], slab_ref[p, cur, slab_rows(ci), :]],
                                   axis=0)
        return nat_ref[p, rows(tail + q_rows(ci, dil) - dil * blk, 2 * blk, dil), :]

    def store_out(pi, p, ci, on, lse):
        dil = ATT_PATTERNS[pi][1]
        if dil == max_dil:
            on16_ref[p, slab_rows(ci), :] = on
            lse16_ref[p, slab_rows(ci), :] = lse
        else:
            on_ref[pi, p, rows(q_rows(ci, dil), blk, dil), :] = on
            lse_ref[pi, p, rows(q_rows(ci, dil), blk, dil), :] = lse

    ones_cols = jnp.ones((2 * blk, LANES), BF16)
    n_combo = L // blk

    def scores(pi, ci, s_ref):
        tsel = jnp.where(ci < ATT_PATTERNS[pi][1], 1, 0)
        for p in range(2):
            qb = load_q(pi, p, ci)
            kb = load_kv(kn_ref, k16_ref, pi, p, ci).astype(BF16)
            for hh in range(2):
                sel = half_lane if hh == 0 else jnp.logical_not(half_lane)
                h = 2 * p + hh
                s_ref[h] = (_dot_nt(jnp.where(sel, qb, 0.0).astype(BF16), kb)
                            + btab_ref[tsel, pi * ATT_HEADS + h])

    def softmax_pv(pi, ci, s_ref):
        for p in range(2):
            vb = jnp.concatenate([load_kv(vn_ref, v16_ref, pi, p, ci).astype(BF16), ones_cols], axis=1)
            mxs, ovs = [], []
            for hh in range(2):
                sc = s_ref[2 * p + hh]
                mx = jnp.max(sc, axis=-1, keepdims=True)
                ovs.append(_dot(jnp.exp2(sc - mx).astype(BF16), vb))
                mxs.append(mx)
            den = jnp.where(half_lane, ovs[0][:, LANES:], ovs[1][:, LANES:])
            num = jnp.where(half_lane, ovs[0][:, :LANES], ovs[1][:, :LANES])
            store_out(pi, p, ci, num * (1.0 / den), jnp.where(half_lane, mxs[0], mxs[1]) + jnp.log2(den))

    sc_refs = (sc0_ref, sc1_ref)
    unroll = ATT_UNROLL
    scores(0, 0, sc0_ref)
    for pi in range(n_pat):
        def group(c0, last, pi=pi):
            for k in range(unroll):
                if not last or k + 1 < unroll:
                    scores(pi, c0 + k + 1, sc_refs[(k + 1) % 2])
                elif pi + 1 < n_pat:
                    scores(pi + 1, 0, sc_refs[(k + 1) % 2])
                softmax_pv(pi, c0 + k, sc_refs[k % 2])

        if n_combo > unroll:
            def body(j, carry, group=group):
                group(j * unroll, False)
                return carry

            lax.fori_loop(0, n_combo // unroll - 1, body, 0)
        group(n_combo - unroll, True)

    def merge(ti, carry):
        t0 = pl.multiple_of(ti * blk, blk)
        g0 = ti * (blk // max_dil)

        def regrouped(ref, p):
            return jnp.concatenate([ref[p, pl.ds(g0 + g, max_dil, stride=p16), :]
                                    for g in range(blk // max_dil)], axis=0)

        for p in range(2):
            ls_ = [lse_ref[pi, p, pl.ds(t0, blk), :] for pi in range(n_pat - 1)] + [regrouped(lse16_ref, p)]
            os_ = [on_ref[pi, p, pl.ds(t0, blk), :] for pi in range(n_pat - 1)] + [regrouped(on16_ref, p)]
            mx = functools.reduce(jnp.maximum, ls_)
            ws = [jnp.exp2(x - mx) for x in ls_]
            num = sum(w * o for w, o in zip(ws, os_))
            o_ref[pl.ds(t0, blk), p * LANES:(p + 1) * LANES] = num / sum(ws)
        return carry

    lax.fori_loop(0, L // blk, merge, 0)


def _attn(z, q_g, k_g, bias):
    bsz, s, _ = z.shape
    L = ATT_L
    assert s % L == 0
    gq = jnp.concatenate([q_g, q_g]).reshape(1, LANES).astype(F32)
    gk = jnp.concatenate([k_g, k_g]).reshape(1, LANES).astype(F32)
    n_pat = len(ATT_PATTERNS)
    slabs = ATT_PATTERNS[-1][1] * ATT_P16

    def zspec(j):
        return pl.BlockSpec((None, L, D_GROUP), lambda b, i, j=j: (b, i, j))

    return pl.pallas_call(
        _attn_kernel,
        grid=(bsz, s // L),
        in_specs=[zspec(7), zspec(8), zspec(9),
                  pl.BlockSpec((1, LANES), lambda b, i: (0, 0)),
                  pl.BlockSpec((1, LANES), lambda b, i: (0, 0)),
                  pl.BlockSpec(bias.shape, lambda b, i: (0, 0, 0))],
        out_specs=pl.BlockSpec((None, L, D_GROUP), lambda b, i: (b, i, 0)),
        out_shape=jax.ShapeDtypeStruct((bsz, s, D_GROUP), F32),
        scratch_shapes=[pltpu.VMEM((2, L, LANES), F32),
                        pltpu.VMEM((2, ATT_TAIL + L, LANES), F32),
                        pltpu.VMEM((2, ATT_TAIL + L, LANES), F32),
                        pltpu.VMEM((2, slabs, LANES), F32),
                        pltpu.VMEM((2, 2, slabs, LANES), F32),
                        pltpu.VMEM((2, 2, slabs, LANES), F32),
                        pltpu.VMEM((n_pat - 1, 2, L, LANES), F32),
                        pltpu.VMEM((n_pat - 1, 2, L, LANES), F32),
                        pltpu.VMEM((2, slabs, LANES), F32),
                        pltpu.VMEM((2, slabs, LANES), F32),
                        pltpu.VMEM((2,) + bias.shape, F32),
                        pltpu.VMEM((ATT_HEADS, ATT_BLOCK, 2 * ATT_BLOCK), F32),
                        pltpu.VMEM((ATT_HEADS, ATT_BLOCK, 2 * ATT_BLOCK), F32)],
        compiler_params=_cparams(("parallel", "arbitrary")),
        name="dilated_attn",
    )(z, z, z, gq, gk, bias)


def _post_kernel(x_ref, ya_ref, yc_ref, yd_ref, val0_ref, gate0_ref, valn_ref, gaten_ref,
                 cw_ref, cb_ref, clg_ref, clb_ref, gmix_ref, wout_ref, gmlp_ref, wup_ref, wdn_ref,
                 o_ref, h_ref, yb_ref):
    conv_params = (cw_ref, cb_ref, clg_ref, clb_ref)

    @pl.when(pl.program_id(1) == 0)
    def _():
        for step in _conv_tile_steps(val0_ref, gate0_ref, *conv_params, yb_ref, h_ref, first=True):
            step()

    conv_steps = _conv_tile_steps(valn_ref, gaten_ref, *conv_params, yb_ref, h_ref, first=False)
    ys = (ya_ref[...], yb_ref[...], yc_ref[...], yd_ref[...])
    conv_steps.pop(0)()
    acc = x_ref[...]
    for j, y in enumerate(ys):
        cs = slice(j * D_GROUP, (j + 1) * D_GROUP)
        yn = y * lax.rsqrt(jnp.mean(y * y, axis=-1, keepdims=True) + EPS) * gmix_ref[:, cs]
        acc = acc + _dot(yn.astype(BF16), wout_ref[cs, :])
    h = (acc * lax.rsqrt(jnp.mean(acc * acc, axis=-1, keepdims=True) + EPS) * gmlp_ref[...]).astype(BF16)
    mlp = jnp.zeros_like(acc)
    ff_tile = D_MODEL
    per_dot = len(conv_steps) * ff_tile // (2 * D_FF)
    for f0 in range(0, D_FF, ff_tile):
        hm = jnp.maximum(_dot(h, wup_ref[:, f0:f0 + ff_tile]), 0.0)
        for _ in range(per_dot):
            conv_steps.pop(0)()
        mlp = mlp + _dot((hm * hm).astype(BF16), wdn_ref[f0:f0 + ff_tile, :])
        for _ in range(per_dot):
            conv_steps.pop(0)()
    assert not conv_steps
    o_ref[...] = acc + mlp


def _post(x, z, y_a, y_c, y_d, conv_w, conv_b, ln_g, ln_b, gmix, gmlp, wout_all, wup_all, wdn_all, layer):
    bsz, s, d = x.shape
    n_tiles = s // ROW_TILE
    row = lambda n: pl.BlockSpec((None, ROW_TILE, n), lambda b, i: (b, i, 0))
    const = lambda shape: pl.BlockSpec(shape, lambda b, i: (0, 0), pipeline_mode=pl.Buffered(1))
    weight = lambda shape: pl.BlockSpec((None,) + shape, lambda b, i: (layer, 0, 0),
                                        pipeline_mode=pl.Buffered(1))
    z_first = lambda j: pl.BlockSpec((None, ROW_TILE, D_GROUP), lambda b, i: (b, 0, j))
    z_next = lambda j: pl.BlockSpec((None, ROW_TILE, D_GROUP),
                                    lambda b, i: (b, jnp.minimum(i + 1, n_tiles - 1), j))
    vec = lambda v: v.reshape(1, -1)
    return pl.pallas_call(
        _post_kernel,
        grid=(bsz, n_tiles),
        in_specs=[row(d), row(D_GROUP), row(D_GROUP), row(D_GROUP),
                  z_first(4), z_first(5), z_next(4), z_next(5),
                  const((CONV_WIDTH, D_GROUP)), const((1, D_GROUP)), const((1, D_GROUP)), const((1, D_GROUP)),
                  const((1, d)), weight((d, d)), const((1, d)), weight((d, D_FF)), weight((D_FF, d))],
        out_specs=row(d),
        out_shape=jax.ShapeDtypeStruct((bsz, s, d), F32),
        scratch_shapes=[pltpu.VMEM((CONV_HALO + ROW_TILE, D_GROUP), F32),
                        pltpu.VMEM((ROW_TILE, D_GROUP), F32)],
        compiler_params=_cparams(("parallel", "arbitrary")),
        name="outproj_mlp",
    )(x, y_a, y_c, y_d, z, z, z, z, conv_w, vec(conv_b), vec(ln_g), vec(ln_b),
      vec(gmix), wout_all, vec(gmlp), wup_all, wdn_all)


def kernel(x, norm_mix_g, w_in, hgrn_lb_logits, hgrn_norm_g, conv_w, conv_b, conv_ln_g, conv_ln_b, s5_lambda_re, s5_lambda_im, s5_b_re, s5_b_im, s5_c_re, s5_c_im, s5_d, s5_log_dt, s5_w_glu, attn_q_norm_g, attn_k_norm_g, rel_bias, mix_out_norm_g, w_out, norm_mlp_g, w_mlp_up, w_mlp_down):
    depth = w_in.shape[0]
    lb_sm = jax.nn.softmax(hgrn_lb_logits.astype(F32), axis=0)
    lb_all = jnp.maximum(jnp.cumsum(lb_sm, axis=0) - lb_sm[0], 0.0)
    bias = _attn_bias(rel_bias)
    w_in, w_out, w_mlp_up, w_mlp_down = (w.astype(BF16) for w in (w_in, w_out, w_mlp_up, w_mlp_down))
    s5_a, s5_bm, s5_cm = jax.vmap(_s5_matrices)(s5_lambda_re, s5_lambda_im, s5_b_re, s5_b_im,
                                                 s5_c_re, s5_c_im, s5_log_dt)
    s5_w_glu = s5_w_glu.astype(BF16)
    for l in range(depth):
        z = _inproj(x, norm_mix_g[l], w_in, l)
        y_a = _hgrn(z, lb_all[l], hgrn_norm_g[l])
        y_c = _s5(z, s5_a, s5_bm, s5_cm, s5_d[l], s5_w_glu, l)
        y_d = _attn(z, attn_q_norm_g[l], attn_k_norm_g[l], bias)
        x = _post(x, z, y_a, y_c, y_d, conv_w[l], conv_b[l], conv_ln_g[l], conv_ln_b[l],
                  mix_out_norm_g[l], norm_mlp_g[l], w_out, w_mlp_up, w_mlp_down, l)
    return x
```

```python
import functools
import math

import numpy as np
import jax
import jax.numpy as jnp
from jax import lax
from jax.experimental import pallas as pl
from jax.experimental.pallas import tpu as pltpu

F32 = jnp.float32
BF16 = jnp.bfloat16

D_MODEL = 1024
D_GROUP = 256
N_MIXERS = 4
HG_HEADS = 4
HG_DK = 64
LB_FLOOR = 1e-30
CONV_WIDTH = 31
S5_CH = 16
S5_GROUPS = 16
S5_STATE = 64
ATT_HEADS = 4
ATT_DH = 64
ATT_PATTERNS = ((128, 1), (512, 4), (2048, 16))
ATT_BLOCK = 128
REL_BUCKETS = 32
REL_MAX_DIST = 2048
NEG_BIG = -1e30
D_FF = 4 * D_MODEL
N_IN_SLICES = 10
D_IN = N_IN_SLICES * D_GROUP
EPS = 1e-6
LOG2E = 1.4426950408889634

LANES = 128
SUBLANES = 8
VMEM_LIMIT = 56 * 1024 * 1024

ROW_TILE = 512
INPROJ_TILE = 2048
HG_CHUNK = 128
HG_BLOCK = 2048
CONV_SUB = 64
CONV_HALO = 32
S5_BLOCK = 512
S5_PITCH = S5_BLOCK + SUBLANES // 2
S5_TILES = 2 * S5_GROUPS * S5_STATE // LANES
ATT_L = 2048
ATT_TAIL = 512
ATT_P16 = ATT_BLOCK + SUBLANES
ATT_UNROLL = 16


def _cparams(sem):
    return pltpu.CompilerParams(dimension_semantics=sem, vmem_limit_bytes=VMEM_LIMIT)


def _sigmoid(x):
    return 1.0 / (1.0 + jnp.exp(-x))


def _dot(a, b):
    return jnp.dot(a, b, preferred_element_type=F32)


def _dot_nt(a, b):
    return lax.dot_general(a, b, (((1,), (1,)), ((), ())), preferred_element_type=F32)


def _dot_tn(a, b):
    return lax.dot_general(a, b, (((0,), (0,)), ((), ())), preferred_element_type=F32)


def _inproj_kernel(x_ref, g_ref, w_ref, z_ref):
    x = x_ref[...]
    h = x * lax.rsqrt(jnp.mean(x * x, axis=-1, keepdims=True) + EPS) * g_ref[...]
    z_ref[...] = _dot(h.astype(BF16), w_ref[...]).astype(z_ref.dtype)


def _inproj(x, g, w_all, layer):
    bsz, s, d = x.shape
    n = w_all.shape[2]
    return pl.pallas_call(
        _inproj_kernel,
        grid=(bsz, s // INPROJ_TILE),
        in_specs=[
            pl.BlockSpec((None, INPROJ_TILE, d), lambda b, i: (b, i, 0)),
            pl.BlockSpec((1, d), lambda b, i: (0, 0)),
            pl.BlockSpec((None, d, n), lambda b, i: (layer, 0, 0)),
        ],
        out_specs=pl.BlockSpec((None, INPROJ_TILE, n), lambda b, i: (b, i, 0)),
        out_shape=jax.ShapeDtypeStruct((bsz, s, n), BF16),
        compiler_params=_cparams(("parallel", "parallel")),
        name="inproj",
    )(x, g.reshape(1, d), w_all)


def _hold(gc, m):
    c, w = gc.shape
    if 2 * m >= SUBLANES:
        x = gc.reshape(c // (2 * m), 2 * m, w)
        return jnp.broadcast_to(x[:, m - 1:m, :], x.shape).reshape(c, w)
    x = gc.reshape(c // SUBLANES, SUBLANES, w)
    rows = lax.broadcasted_iota(jnp.int32, x.shape, 1)
    out = None
    for p in range(SUBLANES // (2 * m)):
        e = p * 2 * m + m - 1
        b = jnp.broadcast_to(x[:, e:e + 1, :], x.shape)
        out = b if out is None else jnp.where(rows >= p * 2 * m, b, out)
    return out.reshape(c, w)


def _hgrn_kernel(q_ref, f_ref, i_ref, g_ref, lb_ref, ng_ref, lvl_ref, o_ref, st_ref):
    @pl.when(pl.program_id(1) == 0)
    def _():
        st_ref[...] = jnp.zeros_like(st_ref)

    c = HG_CHUNK
    w = D_GROUP
    n_levels = int(math.log2(c))
    lane_head = lax.broadcasted_iota(jnp.int32, (c, w), 1) // HG_DK
    row = lax.broadcasted_iota(jnp.int32, (c, w), 0)
    blk = (lax.broadcasted_iota(jnp.int32, (w, w), 0) // HG_DK
           == lax.broadcasted_iota(jnp.int32, (w, w), 1) // HG_DK)
    ones_blk = jnp.where(blk, 1.0, 0.0).astype(BF16)
    lb = jnp.clip(lb_ref[...], 0.0, 1.0 - 1e-6)
    lb_floor = jnp.maximum(lb, LB_FLOOR)
    tri = jnp.where(lax.broadcasted_iota(jnp.int32, (c, c), 0) >= lax.broadcasted_iota(jnp.int32, (c, c), 1),
                    1.0, 0.0).astype(BF16)
    sgn = [jnp.where((row & (2 ** lv)) != 0, 1.0, -1.0) for lv in range(n_levels)]
    grp = SUBLANES

    def stack_heads(t):
        return jnp.concatenate(
            [jnp.where(lane_head == h, t, 0.0).astype(BF16) for h in range(HG_HEADS)], axis=0)

    q_all = q_ref[...].astype(F32)
    gated = (1.0 - lb) * _sigmoid(f_ref[...].astype(F32))
    lf_all = jnp.log2(lb_floor + gated)
    kk_all = (1.0 - lb) - gated
    qk = q_all * kk_all
    qk_hi = qk.astype(BF16)
    qk_lo = (qk - qk_hi.astype(F32)).astype(BF16)
    diag_all = _dot(qk_hi, ones_blk) + _dot(qk_lo, ones_blk)
    hi = lf_all.astype(BF16)
    r1 = lf_all - hi.astype(F32)
    mid = r1.astype(BF16)
    lo = (r1 - mid.astype(F32)).astype(BF16)
    n_chunks = q_ref.shape[0] // c
    cums = _dot(tri, jnp.concatenate([part[j * c:(j + 1) * c, :] for j in range(n_chunks)
                                      for part in (hi, mid, lo)], axis=1))

    def chunk(r0, st):
        q = q_all[r0:r0 + c, :]
        kk = kk_all[r0:r0 + c, :]
        v = i_ref[r0:r0 + c, :].astype(F32)
        g0 = (r0 // c) * 3 * w
        gc = cums[:, g0:g0 + w] + cums[:, g0 + w:g0 + 2 * w] + cums[:, g0 + 2 * w:g0 + 3 * w]

        q_b = q.astype(BF16)
        k_b = [jnp.where(lane_head == h, kk, 0.0).astype(BF16) for h in range(HG_HEADS)]
        n_grp = c // grp
        scores = [None] * n_grp
        for lv in range(n_levels):
            m = 2 ** lv
            e = jnp.exp2((gc - _hold(gc, m)) * sgn[lv]).astype(BF16)
            p_lv = _dot_nt(q_b * e, jnp.concatenate([kb * e for kb in k_b], axis=0))
            for gi in range(n_grp):
                if m >= grp and (gi * grp) & m == 0:
                    continue
                rs = slice(gi * grp, (gi + 1) * grp)
                old = 0.0 if scores[gi] is None else scores[gi]
                scores[gi] = jnp.where(lvl_ref[rs, :] == lv, p_lv[rs], old)
        o = _dot(jnp.concatenate(scores, axis=0).astype(BF16), stack_heads(v))
        o = o + diag_all[r0:r0 + c, :] * v
        o = o + _dot_nt((q * jnp.exp2(gc)).astype(BF16), st.astype(BF16))
        g_last = gc[c - 1:c, :]
        upd = _dot_tn(v.astype(BF16), (kk * jnp.exp2(g_last - gc)).astype(BF16))
        return o, st * jnp.exp2(g_last) + jnp.where(blk, upd, 0.0)

    st = st_ref[...]
    outs = []
    for r0 in range(0, q_ref.shape[0], c):
        o, st = chunk(r0, st)
        outs.append(o)
    st_ref[...] = st

    o_all = jnp.concatenate(outs, axis=0)
    ms = _dot((o_all * o_all).astype(BF16), ones_blk) * (1.0 / HG_DK)
    gate = g_ref[...].astype(F32)
    o_ref[...] = o_all * lax.rsqrt(ms + EPS) * ng_ref[...] * (gate * _sigmoid(gate))


def _hgrn_levels():
    t = np.arange(HG_CHUNK)[:, None]
    s = np.arange(HG_CHUNK)[None, :]
    x = np.maximum(t ^ s, 1)
    lv = np.where(s < t, np.floor(np.log2(x)).astype(np.int32), -1).astype(np.int32)
    return np.tile(lv, (1, HG_HEADS))


def _hgrn(z, lb, norm_g):
    bsz, s, _ = z.shape
    tb = min(HG_BLOCK, s)

    def zspec(j):
        return pl.BlockSpec((None, tb, D_GROUP), lambda b, i, j=j: (b, i, j))

    const = lambda shape: pl.BlockSpec(shape, lambda b, i: (0, 0))
    return pl.pallas_call(
        _hgrn_kernel,
        grid=(bsz, s // tb),
        in_specs=[zspec(0), zspec(1), zspec(2), zspec(3),
                  const((1, D_GROUP)), const((1, D_GROUP)), const((HG_CHUNK, HG_HEADS * HG_CHUNK))],
        out_specs=pl.BlockSpec((None, tb, D_GROUP), lambda b, i: (b, i, 0)),
        out_shape=jax.ShapeDtypeStruct((bsz, s, D_GROUP), F32),
        scratch_shapes=[pltpu.VMEM((D_GROUP, D_GROUP), F32)],
        compiler_params=_cparams(("parallel", "arbitrary")),
        name="hgrn2",
    )(z, z, z, z, lb.reshape(1, D_GROUP), norm_g.reshape(1, D_GROUP), jnp.asarray(_hgrn_levels()))


def _conv_tile_steps(val_ref, gate_ref, w_ref, b_ref, lg_ref, lbias_ref, o_ref, h_ref, first):
    tb = val_ref.shape[0]
    base = CONV_HALO - (CONV_WIDTH - 1)

    def glu():
        if first:
            h_ref[0:CONV_HALO, :] = jnp.zeros((CONV_HALO, D_GROUP), F32)
        else:
            h_ref[0:CONV_HALO, :] = h_ref[tb:tb + CONV_HALO, :]
        h_ref[CONV_HALO:CONV_HALO + tb, :] = val_ref[...].astype(F32) * _sigmoid(gate_ref[...].astype(F32))

    def sub_tile(t0):
        acc = jnp.broadcast_to(b_ref[...], (CONV_SUB, D_GROUP))
        for c in range(SUBLANES):
            part = None
            rows = CONV_SUB + (SUBLANES if c else 0)
            for j in range(CONV_WIDTH):
                if (base + j) % SUBLANES != c:
                    continue
                r0 = t0 + base + j - c
                term = w_ref[j:j + 1, :] * h_ref[r0:r0 + rows, :]
                part = term if part is None else part + term
            acc = acc + part[c:c + CONV_SUB, :]
        mu = jnp.mean(acc, axis=-1, keepdims=True)
        d = acc - mu
        var = jnp.mean(d * d, axis=-1, keepdims=True)
        y = d * lax.rsqrt(var + EPS) * lg_ref[...] + lbias_ref[...]
        o_ref[t0:t0 + CONV_SUB, :] = y * _sigmoid(y)

    return [glu] + [functools.partial(sub_tile, t0) for t0 in range(0, tb, CONV_SUB)]


def _s5_kernel(u_ref, a_ref, bm_ref, cm_ref, d_ref, wg_ref, o_ref, bu_ref, hs_ref, st_ref):
    nb, t_len, _ = u_ref.shape
    half = S5_TILES // 2

    @pl.when(pl.program_id(0) == 0)
    def _():
        st_ref[...] = jnp.zeros_like(st_ref)

    ar = a_ref[0:half, :]
    ai = a_ref[half:S5_TILES, :]

    for b in range(nb):
        bu = _dot(u_ref[b], bm_ref[...])
        for i in range(S5_TILES):
            bu_ref[b, i * S5_PITCH:i * S5_PITCH + t_len, :] = bu[:, i * LANES:(i + 1) * LANES]

    hs = [(st_ref[b, 0:half, :], st_ref[b, half:S5_TILES, :]) for b in range(nb)]
    for t in range(t_len):
        for b in range(nb):
            hr, hi = hs[b]
            br = bu_ref[b, pl.ds(t, half, stride=S5_PITCH), :]
            bi = bu_ref[b, pl.ds(half * S5_PITCH + t, half, stride=S5_PITCH), :]
            nr = ar * hr - ai * hi + br
            ni = ar * hi + ai * hr + bi
            hs_ref[b, pl.ds(t, half, stride=S5_PITCH), :] = nr
            hs_ref[b, pl.ds(half * S5_PITCH + t, half, stride=S5_PITCH), :] = ni
            hs[b] = (nr, ni)
    for b in range(nb):
        st_ref[b, 0:half, :] = hs[b][0]
        st_ref[b, half:S5_TILES, :] = hs[b][1]

    hall = jnp.concatenate([jnp.concatenate(
        [hs_ref[b, i * S5_PITCH:i * S5_PITCH + t_len, :].astype(BF16) for i in range(S5_TILES)], axis=1)
        for b in range(nb)], axis=0)
    uall = jnp.concatenate([u_ref[b] for b in range(nb)], axis=0).astype(F32)
    y = _dot(hall, cm_ref[...]) + d_ref[...] * uall
    y = 0.5 * y * (1.0 + jnp.tanh(math.sqrt(2.0 / math.pi) * (y + 0.044715 * (y * y * y))))
    y = y * _sigmoid(_dot(y.astype(BF16), wg_ref[...]))
    for b in range(nb):
        o_ref[b] = y[b * t_len:(b + 1) * t_len, :]


def _s5_matrices(lam_re, lam_im, b_re, b_im, c_re, c_im, log_dt):
    dt = jnp.exp(log_dt.astype(F32))[:, None]
    lr = lam_re.astype(F32)
    li = lam_im.astype(F32)
    mag = jnp.exp(lr * dt)
    a_re = mag * jnp.cos(li * dt)
    a_im = mag * jnp.sin(li * dt)
    den = lr * lr + li * li
    coef_re = ((a_re - 1.0) * lr + a_im * li) / den
    coef_im = (a_im * lr - (a_re - 1.0) * li) / den
    br = b_re.astype(F32)
    bi = b_im.astype(F32)
    bbar_re = coef_re[..., None] * br - coef_im[..., None] * bi
    bbar_im = coef_re[..., None] * bi + coef_im[..., None] * br
    eye = jnp.eye(S5_GROUPS, dtype=F32)
    n_state = S5_GROUPS * S5_STATE

    def b_block(bb):
        return jnp.einsum('gpc,gh->gchp', bb, eye).reshape(D_GROUP, n_state)

    def c_block(cc):
        return jnp.einsum('gcp,gh->gphc', cc, eye).reshape(n_state, D_GROUP)

    bm = jnp.concatenate([b_block(bbar_re), b_block(bbar_im)], axis=1).astype(BF16)
    cm = jnp.concatenate([c_block(c_re.astype(F32)), -c_block(c_im.astype(F32))], axis=0).astype(BF16)
    a = jnp.concatenate([a_re.reshape(n_state // LANES, LANES), a_im.reshape(n_state // LANES, LANES)], 0)
    return a, bm, cm


def _s5(z, a_all, bm_all, cm_all, d_skip, wglu_all, layer):
    bsz, s, _ = z.shape
    tb = min(S5_BLOCK, s)
    assert tb == S5_BLOCK
    const = lambda shape: pl.BlockSpec(shape, lambda i: (0, 0))
    per_layer = lambda shape: pl.BlockSpec((None,) + shape, lambda i: (layer, 0, 0))
    n_state2 = 2 * S5_GROUPS * S5_STATE
    return pl.pallas_call(
        _s5_kernel,
        grid=(s // tb,),
        in_specs=[pl.BlockSpec((bsz, tb, D_GROUP), lambda i: (0, i, 6)),
                  per_layer((S5_TILES, LANES)), per_layer((D_GROUP, n_state2)), per_layer((n_state2, D_GROUP)),
                  const((1, D_GROUP)), per_layer((D_GROUP, D_GROUP))],
        out_specs=pl.BlockSpec((bsz, tb, D_GROUP), lambda i: (0, i, 0)),
        out_shape=jax.ShapeDtypeStruct((bsz, s, D_GROUP), F32),
        scratch_shapes=[pltpu.VMEM((bsz, S5_TILES * S5_PITCH, LANES), F32),
                        pltpu.VMEM((bsz, S5_TILES * S5_PITCH, LANES), F32),
                        pltpu.VMEM((bsz, S5_TILES, LANES), F32)],
        compiler_params=_cparams(("arbitrary",)),
        name="s5",
    )(z, a_all, bm_all, cm_all, d_skip.reshape(1, D_GROUP), wglu_all)


def _t5_bucket_np(dist):
    max_exact = REL_BUCKETS // 2
    distf = np.maximum(dist, max_exact).astype(np.float32)
    large = max_exact + (np.log(distf / np.float32(max_exact)) / np.float32(math.log(REL_MAX_DIST / max_exact))
                         * np.float32(REL_BUCKETS - max_exact)).astype(np.int32)
    large = np.minimum(large, REL_BUCKETS - 1)
    return np.where(dist < max_exact, dist, large)


def _attn_bias(rel_bias):
    blk = ATT_BLOCK
    period = 3 * blk + 1
    k = np.arange(-(blk - 1), 2 * blk)
    tabs = []
    for window, dilation in ATT_PATTERNS:
        delta = blk - k
        valid = (delta >= 0) & (delta <= window // dilation)
        bucket = _t5_bucket_np(np.maximum(delta, 0) * dilation)
        vals = jnp.where(valid[:, None], rel_bias.astype(F32)[bucket] * LOG2E, NEG_BIG)
        v = jnp.full((period, ATT_HEADS), NEG_BIG, F32).at[k % period].set(vals).T
        tiled = jnp.tile(v, (1, blk))[:, :blk * (period - 1)]
        tabs.append(tiled.reshape(ATT_HEADS, blk, period - 1)[:, :, :2 * blk])
    return jnp.concatenate(tabs, axis=0)


def _attn_kernel(q_ref, k_ref, v_ref, qg_ref, kg_ref, bias_ref, o_ref,
                 qn_ref, kn_ref, vn_ref, q16_ref, k16_ref, v16_ref,
                 on_ref, lse_ref, on16_ref, lse16_ref, btab_ref, sc0_ref, sc1_ref):
    L = q_ref.shape[0]
    blk = ATT_BLOCK
    n_pat = len(ATT_PATTERNS)
    max_dil = ATT_PATTERNS[-1][1]
    tail = ATT_TAIL
    p16 = ATT_P16
    step = pl.program_id(1)
    first = step == 0
    cur = step % 2
    half_lane = lax.broadcasted_iota(jnp.int32, (blk, LANES), 1) < ATT_DH
    mean_blk = jnp.where(
        lax.broadcasted_iota(jnp.int32, (LANES, LANES), 0) // ATT_DH
        == lax.broadcasted_iota(jnp.int32, (LANES, LANES), 1) // ATT_DH, 1.0 / ATT_DH, 0.0).astype(BF16)

    @pl.when(first)
    def _():
        kn_ref[:, 0:tail, :] = jnp.zeros((2, tail, LANES), F32)
        vn_ref[:, 0:tail, :] = jnp.zeros((2, tail, LANES), F32)
        k16_ref[...] = jnp.zeros_like(k16_ref)
        v16_ref[...] = jnp.zeros_like(v16_ref)
        prev_cols = lax.broadcasted_iota(jnp.int32, (blk, 2 * blk), 1) < blk
        for t in range(n_pat * ATT_HEADS):
            btab_ref[0, t] = bias_ref[t]
            btab_ref[1, t] = jnp.where(prev_cols, NEG_BIG, bias_ref[t])

    @pl.when(step == 1)
    def _():
        for t in range(n_pat * ATT_HEADS):
            btab_ref[1, t] = bias_ref[t]

    @pl.when(jnp.logical_not(first))
    def _():
        kn_ref[:, 0:tail, :] = kn_ref[:, L:L + tail, :]
        vn_ref[:, 0:tail, :] = vn_ref[:, L:L + tail, :]

    def head_norm(x, g):
        ms = _dot((x * x).astype(BF16), mean_blk)
        return x * (lax.rsqrt(ms + EPS) * g)

    q_gain = qg_ref[...] * (ATT_DH ** -0.5 * LOG2E)
    for p in range(2):
        ls = slice(p * LANES, (p + 1) * LANES)
        qn_ref[p] = head_norm(q_ref[:, ls].astype(F32), q_gain)
        kn_ref[p, tail:tail + L, :] = head_norm(k_ref[:, ls].astype(F32), kg_ref[...])
        vn_ref[p, tail:tail + L, :] = v_ref[:, ls].astype(F32)

    def regroup(n, carry):
        src = pl.multiple_of(n * max_dil, max_dil)
        dst = pl.ds(n, max_dil, stride=p16)
        for p in range(2):
            q16_ref[p, dst, :] = qn_ref[p, pl.ds(src, max_dil), :]
            k16_ref[p, cur, dst, :] = kn_ref[p, pl.ds(tail + src, max_dil), :]
            v16_ref[p, cur, dst, :] = vn_ref[p, pl.ds(tail + src, max_dil), :]
        return carry

    lax.fori_loop(0, L // max_dil, regroup, 0, unroll=4)

    def rows(start, n, dil):
        return pl.ds(start, n) if dil == 1 else pl.ds(start, n, stride=dil)

    def q_rows(ci, dil):
        return (ci % dil) + dil * blk * (ci // dil)

    def slab_rows(ci):
        r0 = ci * p16
        return pl.ds(r0 if isinstance(r0, int) else pl.multiple_of(r0, SUBLANES), blk)

    def load_q(pi, p, ci):
        dil = ATT_PATTERNS[pi][1]
        if dil == max_dil:
            return q16_ref[p, slab_rows(ci), :]
        return qn_ref[p, rows(q_rows(ci, dil), blk, dil), :]

    def load_kv(nat_ref, slab_ref, pi, p, ci):
        dil = ATT_PATTERNS[pi][1]
        if dil == max_dil:
            return jnp.concatenate([slab_ref[p, 1 - cur, slab_rows(ci), :], slab_ref[p, cur, slab_rows(ci), :]],
                                   axis=0)
        return nat_ref[p, rows(tail + q_rows(ci, dil) - dil * blk, 2 * blk, dil), :]

    def store_out(pi, p, ci, on, lse):
        dil = ATT_PATTERNS[pi][1]
        if dil == max_dil:
            on16_ref[p, slab_rows(ci), :] = on
            lse16_ref[p, slab_rows(ci), :] = lse
        else:
            on_ref[pi, p, rows(q_rows(ci, dil), blk, dil), :] = on
            lse_ref[pi, p, rows(q_rows(ci, dil), blk, dil), :] = lse

    ones_cols = jnp.ones((2 * blk, LANES), BF16)
    n_combo = L // blk

    def scores(pi, ci, s_ref):
        tsel = jnp.where(ci < ATT_PATTERNS[pi][1], 1, 0)
        for p in range(2):
            qb = load_q(pi, p, ci)
            kb = load_kv(kn_ref, k16_ref, pi, p, ci).astype(BF16)
            for hh in range(2):
                sel = half_lane if hh == 0 else jnp.logical_not(half_lane)
                h = 2 * p + hh
                s_ref[h] = (_dot_nt(jnp.where(sel, qb, 0.0).astype(BF16), kb)
                            + btab_ref[tsel, pi * ATT_HEADS + h])

    def softmax_pv(pi, ci, s_ref):
        for p in range(2):
            vb = jnp.concatenate([load_kv(vn_ref, v16_ref, pi, p, ci).astype(BF16), ones_cols], axis=1)
            mxs, ovs = [], []
            for hh in range(2):
                sc = s_ref[2 * p + hh]
                mx = jnp.max(sc, axis=-1, keepdims=True)
                ovs.append(_dot(jnp.exp2(sc - mx).astype(BF16), vb))
                mxs.append(mx)
            den = jnp.where(half_lane, ovs[0][:, LANES:], ovs[1][:, LANES:])
            num = jnp.where(half_lane, ovs[0][:, :LANES], ovs[1][:, :LANES])
            store_out(pi, p, ci, num * (1.0 / den), jnp.where(half_lane, mxs[0], mxs[1]) + jnp.log2(den))

    sc_refs = (sc0_ref, sc1_ref)
    unroll = ATT_UNROLL
    scores(0, 0, sc0_ref)
    for pi in range(n_pat):
        def group(c0, last, pi=pi):
            for k in range(unroll):
                if not last or k + 1 < unroll:
                    scores(pi, c0 + k + 1, sc_refs[(k + 1) % 2])
                elif pi + 1 < n_pat:
                    scores(pi + 1, 0, sc_refs[(k + 1) % 2])
                softmax_pv(pi, c0 + k, sc_refs[k % 2])

        if n_combo > unroll:
            def body(j, carry, group=group):
                group(j * unroll, False)
                return carry

            lax.fori_loop(0, n_combo // unroll - 1, body, 0)
        group(n_combo - unroll, True)

    def merge(ti, carry):
        t0 = pl.multiple_of(ti * blk, blk)
        g0 = ti * (blk // max_dil)

        def regrouped(ref, p):
            return jnp.concatenate([ref[p, pl.ds(g0 + g, max_dil, stride=p16), :]
                                    for g in range(blk // max_dil)], axis=0)

        for p in range(2):
            ls_ = [lse_ref[pi, p, pl.ds(t0, blk), :] for pi in range(n_pat - 1)] + [regrouped(lse16_ref, p)]
            os_ = [on_ref[pi, p, pl.ds(t0, blk), :] for pi in range(n_pat - 1)] + [regrouped(on16_ref, p)]
            mx = functools.reduce(jnp.maximum, ls_)
            ws = [jnp.exp2(x - mx) for x in ls_]
            num = sum(w * o for w, o in zip(ws, os_))
            o_ref[pl.ds(t0, blk), p * LANES:(p + 1) * LANES] = num / sum(ws)
        return carry

    lax.fori_loop(0, L // blk, merge, 0)


def _attn(z, q_g, k_g, bias):
    bsz, s, _ = z.shape
    L = ATT_L
    assert s % L == 0
    gq = jnp.concatenate([q_g, q_g]).reshape(1, LANES).astype(F32)
    gk = jnp.concatenate([k_g, k_g]).reshape(1, LANES).astype(F32)
    n_pat = len(ATT_PATTERNS)
    slabs = ATT_PATTERNS[-1][1] * ATT_P16

    def zspec(j):
        return pl.BlockSpec((None, L, D_GROUP), lambda b, i, j=j: (b, i, j))

    return pl.pallas_call(
        _attn_kernel,
        grid=(bsz, s // L),
        in_specs=[zspec(7), zspec(8), zspec(9),
                  pl.BlockSpec((1, LANES), lambda b, i: (0, 0)),
                  pl.BlockSpec((1, LANES), lambda b, i: (0, 0)),
                  pl.BlockSpec(bias.shape, lambda b, i: (0, 0, 0))],
        out_specs=pl.BlockSpec((None, L, D_GROUP), lambda b, i: (b, i, 0)),
        out_shape=jax.ShapeDtypeStruct((bsz, s, D_GROUP), F32),
        scratch_shapes=[pltpu.VMEM((2, L, LANES), F32),
                        pltpu.VMEM((2, ATT_TAIL + L, LANES), F32),
                        pltpu.VMEM((2, ATT_TAIL + L, LANES), F32),
                        pltpu.VMEM((2, slabs, LANES), F32),
                        pltpu.VMEM((2, 2, slabs, LANES), F32),
                        pltpu.VMEM((2, 2, slabs, LANES), F32),
                        pltpu.VMEM((n_pat - 1, 2, L, LANES), F32),
                        pltpu.VMEM((n_pat - 1, 2, L, LANES), F32),
                        pltpu.VMEM((2, slabs, LANES), F32),
                        pltpu.VMEM((2, slabs, LANES), F32),
                        pltpu.VMEM((2,) + bias.shape, F32),
                        pltpu.VMEM((ATT_HEADS, ATT_BLOCK, 2 * ATT_BLOCK), F32),
                        pltpu.VMEM((ATT_HEADS, ATT_BLOCK, 2 * ATT_BLOCK), F32)],
        compiler_params=_cparams(("parallel", "arbitrary")),
        name="dilated_attn",
    )(z, z, z, gq, gk, bias)


def _post_kernel(x_ref, ya_ref, yc_ref, yd_ref, val0_ref, gate0_ref, valn_ref, gaten_ref,
                 cw_ref, cb_ref, clg_ref, clb_ref, gmix_ref, wout_ref, gmlp_ref, wup_ref, wdn_ref,
                 o_ref, h_ref, yb_ref):
    conv_params = (cw_ref, cb_ref, clg_ref, clb_ref)

    @pl.when(pl.program_id(1) == 0)
    def _():
        for step in _conv_tile_steps(val0_ref, gate0_ref, *conv_params, yb_ref, h_ref, first=True):
            step()

    conv_steps = _conv_tile_steps(valn_ref, gaten_ref, *conv_params, yb_ref, h_ref, first=False)
    ys = (ya_ref[...], yb_ref[...], yc_ref[...], yd_ref[...])
    conv_steps.pop(0)()
    yn = []
    for j, y in enumerate(ys):
        cs = slice(j * D_GROUP, (j + 1) * D_GROUP)
        yn.append((y * lax.rsqrt(jnp.mean(y * y, axis=-1, keepdims=True) + EPS) * gmix_ref[:, cs]).astype(BF16))
    acc = x_ref[...] + _dot(jnp.concatenate(yn, axis=1), wout_ref[...])
    h = (acc * lax.rsqrt(jnp.mean(acc * acc, axis=-1, keepdims=True) + EPS) * gmlp_ref[...]).astype(BF16)
    mlp = jnp.zeros_like(acc)
    ff_tile = 2 * D_MODEL
    per_dot = len(conv_steps) * ff_tile // (2 * D_FF)
    for f0 in range(0, D_FF, ff_tile):
        hm = jnp.maximum(_dot(h, wup_ref[:, f0:f0 + ff_tile]), 0.0)
        for _ in range(per_dot):
            conv_steps.pop(0)()
        mlp = mlp + _dot((hm * hm).astype(BF16), wdn_ref[f0:f0 + ff_tile, :])
        for _ in range(per_dot):
            conv_steps.pop(0)()
    assert not conv_steps
    o_ref[...] = acc + mlp


def _post(x, z, y_a, y_c, y_d, conv_w, conv_b, ln_g, ln_b, gmix, gmlp, wout_all, wup_all, wdn_all, layer):
    bsz, s, d = x.shape
    n_tiles = s // ROW_TILE
    row = lambda n: pl.BlockSpec((None, ROW_TILE, n), lambda b, i: (b, i, 0))
    const = lambda shape: pl.BlockSpec(shape, lambda b, i: (0, 0), pipeline_mode=pl.Buffered(1))
    weight = lambda shape: pl.BlockSpec((None,) + shape, lambda b, i: (layer, 0, 0),
                                        pipeline_mode=pl.Buffered(1))
    z_first = lambda j: pl.BlockSpec((None, ROW_TILE, D_GROUP), lambda b, i: (b, 0, j))
    z_next = lambda j: pl.BlockSpec((None, ROW_TILE, D_GROUP),
                                    lambda b, i: (b, jnp.minimum(i + 1, n_tiles - 1), j))
    vec = lambda v: v.reshape(1, -1)
    return pl.pallas_call(
        _post_kernel,
        grid=(bsz, n_tiles),
        in_specs=[row(d), row(D_GROUP), row(D_GROUP), row(D_GROUP),
                  z_first(4), z_first(5), z_next(4), z_next(5),
                  const((CONV_WIDTH, D_GROUP)), const((1, D_GROUP)), const((1, D_GROUP)), const((1, D_GROUP)),
                  const((1, d)), weight((d, d)), const((1, d)), weight((d, D_FF)), weight((D_FF, d))],
        out_specs=row(d),
        out_shape=jax.ShapeDtypeStruct((bsz, s, d), F32),
        scratch_shapes=[pltpu.VMEM((CONV_HALO + ROW_TILE, D_GROUP), F32),
                        pltpu.VMEM((ROW_TILE, D_GROUP), F32)],
        compiler_params=_cparams(("parallel", "arbitrary")),
        name="outproj_mlp",
    )(x, y_a, y_c, y_d, z, z, z, z, conv_w, vec(conv_b), vec(ln_g), vec(ln_b),
      vec(gmix), wout_all, vec(gmlp), wup_all, wdn_all)


def kernel(x, norm_mix_g, w_in, hgrn_lb_logits, hgrn_norm_g, conv_w, conv_b, conv_ln_g, conv_ln_b, s5_lambda_re, s5_lambda_im, s5_b_re, s5_b_im, s5_c_re, s5_c_im, s5_d, s5_log_dt, s5_w_glu, attn_q_norm_g, attn_k_norm_g, rel_bias, mix_out_norm_g, w_out, norm_mlp_g, w_mlp_up, w_mlp_down):
    depth = w_in.shape[0]
    lb_sm = jax.nn.softmax(hgrn_lb_logits.astype(F32), axis=0)
    lb_all = jnp.maximum(jnp.cumsum(lb_sm, axis=0) - lb_sm[0], 0.0)
    bias = _attn_bias(rel_bias)
    w_in, w_out, w_mlp_up, w_mlp_down = (w.astype(BF16) for w in (w_in, w_out, w_mlp_up, w_mlp_down))
    s5_a, s5_bm, s5_cm = jax.vmap(_s5_matrices)(s5_lambda_re, s5_lambda_im, s5_b_re, s5_b_im,
                                                 s5_c_re, s5_c_im, s5_log_dt)
    s5_w_glu = s5_w_glu.astype(BF16)
    for l in range(depth):
        z = _inproj(x, norm_mix_g[l], w_in, l)
        y_a = _hgrn(z, lb_all[l], hgrn_norm_g[l])
        y_c = _s5(z, s5_a, s5_bm, s5_cm, s5_d[l], s5_w_glu, l)
        y_d = _attn(z, attn_q_norm_g[l], attn_k_norm_g[l], bias)
        x = _post(x, z, y_a, y_c, y_d, conv_w[l], conv_b[l], conv_ln_g[l], conv_ln_b[l],
                  mix_out_norm_g[l], norm_mlp_g[l], w_out, w_mlp_up, w_mlp_down, l)
    return x
```
